```python
import math
import jax, jax.numpy as jnp
from jax import lax
import numpy as np

D_MODEL = 1024
BATCH = 16
SEQ = 2048
DEPTH = 1

D_MIX = D_MODEL
D_SSM = D_MIX // 2
SSM_GROUP = 16
N_SSM_GROUPS = D_SSM // SSM_GROUP
SSM_STATE = 64
D_ATTN = D_MIX - D_SSM
HEAD_DIM = 64
N_HEADS = D_ATTN // HEAD_DIM
Q_BLOCK = 128
D_FF = 2752
CONV_WIDTH = 3
EPS = 1e-6
DT_MIN = 1e-3
DT_MAX = 1e-1
D_IN_PROJ = D_SSM + 3 * D_ATTN + N_HEADS

kernel_name = "hymba_s5_fox_convffn_block"


def rmsnorm(x, g):
    xf = x.astype(jnp.float32)
    y = xf * lax.rsqrt(jnp.mean(xf * xf, axis=-1, keepdims=True) + EPS)
    return (y * g.astype(jnp.float32)).astype(x.dtype)


def s5_mixer(u, lam_re, lam_im, b_re, b_im, c_re, c_im, d_skip, log_dt, w_glu, b_glu):
    f32 = jnp.float32
    bsz, L, _ = u.shape
    uf = u.astype(f32).reshape(bsz, L, N_SSM_GROUPS, SSM_GROUP)
    lr = lam_re.astype(f32)
    li = lam_im.astype(f32)
    dt = jnp.exp(log_dt.astype(f32))[:, None]
    mag = jnp.exp(lr * dt)
    ab_re = mag * jnp.cos(li * dt)
    ab_im = mag * jnp.sin(li * dt)
    nr = ab_re - 1.0
    ni = ab_im
    den = lr * lr + li * li
    q_re = (nr * lr + ni * li) / den
    q_im = (ni * lr - nr * li) / den
    br = b_re.astype(f32)
    bi = b_im.astype(f32)
    bb_re = q_re[..., None] * br - q_im[..., None] * bi
    bb_im = q_re[..., None] * bi + q_im[..., None] * br
    bu_re = jnp.einsum('blgh,gph->blgp', uf, bb_re)
    bu_im = jnp.einsum('blgh,gph->blgp', uf, bb_im)
    a_re = jnp.broadcast_to(ab_re, (1, L) + ab_re.shape)
    a_im = jnp.broadcast_to(ab_im, (1, L) + ab_im.shape)

    def combine(e1, e2):
        a1r, a1i, b1r, b1i = e1
        a2r, a2i, b2r, b2i = e2
        return (a2r * a1r - a2i * a1i,
                a2r * a1i + a2i * a1r,
                a2r * b1r - a2i * b1i + b2r,
                a2r * b1i + a2i * b1r + b2i)

    _, _, xr, xi = lax.associative_scan(combine, (a_re, a_im, bu_re, bu_im), axis=1)
    y = (jnp.einsum('blgp,ghp->blgh', xr, c_re.astype(f32))
         - jnp.einsum('blgp,ghp->blgh', xi, c_im.astype(f32))
         + d_skip.astype(f32).reshape(N_SSM_GROUPS, SSM_GROUP) * uf)
    y = jax.nn.gelu(y.reshape(bsz, L, D_SSM))
    return y * jax.nn.sigmoid(y @ w_glu.astype(f32) + b_glu.astype(f32))


def fox_attention(q, k, v, f_logit):
    f32 = jnp.float32
    L = q.shape[1]
    log_f = jax.nn.log_sigmoid(f_logit.astype(f32))
    cum = jnp.cumsum(log_f, axis=1).transpose(0, 2, 1)
    scale = HEAD_DIM ** -0.5
    outs = []
    for i in range(L // Q_BLOCK):
        q0 = i * Q_BLOCK
        kend = q0 + Q_BLOCK
        qb = q[:, q0:kend]
        kb = k[:, :kend]
        vb = v[:, :kend]
        s = jnp.einsum('bqhd,bkhd->bhqk', qb, kb).astype(f32) * scale
        s = s + cum[:, :, q0:kend, None] - cum[:, :, None, :kend]
        mask = (q0 + jnp.arange(Q_BLOCK))[:, None] >= jnp.arange(kend)[None, :]
        s = jnp.where(mask, s, -jnp.inf)
        p = jax.nn.softmax(s, axis=-1)
        outs.append(jnp.einsum('bhqk,bkhd->bqhd', p.astype(v.dtype), vb))
    return jnp.concatenate(outs, axis=1)


def conv_ffn(h, w_up, conv_w, conv_b, w_down):
    u = h @ w_up
    L = u.shape[1]
    up = jnp.pad(u, ((0, 0), (CONV_WIDTH - 1, 0), (0, 0)))
    c = up[:, 0:L] * conv_w[0]
    for kk in range(1, CONV_WIDTH):
        c = c + up[:, kk:kk + L] * conv_w[kk]
    c = c + conv_b
    gate, val = jnp.split(c, 2, axis=-1)
    return (jax.nn.silu(gate) * val) @ w_down


def setup_inputs(seed: int = 0) -> dict:
    key = jax.random.key(seed)
    ks = jax.random.split(key, 24)
    f32 = jnp.float32
    nrm = lambda k, shape, s: jax.random.normal(k, shape, f32) * s
    G, P, H = N_SSM_GROUPS, SSM_STATE, SSM_GROUP
    x = jax.random.normal(ks[0], (BATCH, SEQ, D_MODEL), f32)
    norm_mix = 1.0 + nrm(ks[1], (DEPTH, D_MODEL), 0.02)
    w_in = nrm(ks[2], (DEPTH, D_MODEL, D_IN_PROJ), D_MODEL ** -0.5)
    b_forget = jax.random.uniform(ks[3], (DEPTH, N_HEADS), f32, 1.0, 5.0)
    lam_re = -0.5 + nrm(ks[4], (DEPTH, G, P), 0.01)
    lam_im = jnp.pi * jnp.arange(P, dtype=f32)[None, None, :] + nrm(ks[5], (DEPTH, G, P), 0.01)
    b_re = nrm(ks[6], (DEPTH, G, P, H), (2.0 * H) ** -0.5)
    b_im = nrm(ks[7], (DEPTH, G, P, H), (2.0 * H) ** -0.5)
    c_re = nrm(ks[8], (DEPTH, G, H, P), (2.0 * P) ** -0.5)
    c_im = nrm(ks[9], (DEPTH, G, H, P), (2.0 * P) ** -0.5)
    d_skip = nrm(ks[10], (DEPTH, D_SSM), 1.0)
    log_dt = jax.random.uniform(ks[11], (DEPTH, G), f32, math.log(DT_MIN), math.log(DT_MAX))
    w_glu = nrm(ks[12], (DEPTH, D_SSM, D_SSM), D_SSM ** -0.5)
    b_glu = nrm(ks[13], (DEPTH, D_SSM), 0.01)
    q_norm = 1.0 + nrm(ks[14], (DEPTH, HEAD_DIM), 0.02)
    k_norm = 1.0 + nrm(ks[15], (DEPTH, HEAD_DIM), 0.02)
    norm_out_ssm = 1.0 + nrm(ks[16], (DEPTH, D_SSM), 0.02)
    norm_out_attn = 1.0 + nrm(ks[17], (DEPTH, D_ATTN), 0.02)
    w_out = nrm(ks[18], (DEPTH, D_MIX, D_MODEL), D_MIX ** -0.5)
    norm_ffn = 1.0 + nrm(ks[19], (DEPTH, D_MODEL), 0.02)
    w_up = nrm(ks[20], (DEPTH, D_MODEL, 2 * D_FF), D_MODEL ** -0.5)
    conv_w = nrm(ks[21], (DEPTH, CONV_WIDTH, 2 * D_FF), CONV_WIDTH ** -0.5)
    conv_b = nrm(ks[22], (DEPTH, 2 * D_FF), 0.01)
    w_down = nrm(ks[23], (DEPTH, D_FF, D_MODEL), D_FF ** -0.5)
    return {"x": x, "norm_mix": norm_mix, "w_in": w_in, "b_forget": b_forget,
            "lam_re": lam_re, "lam_im": lam_im, "b_re": b_re, "b_im": b_im,
            "c_re": c_re, "c_im": c_im, "d_skip": d_skip, "log_dt": log_dt,
            "w_glu": w_glu, "b_glu": b_glu, "q_norm": q_norm, "k_norm": k_norm,
            "norm_out_ssm": norm_out_ssm, "norm_out_attn": norm_out_attn,
            "w_out": w_out, "norm_ffn": norm_ffn, "w_up": w_up, "conv_w": conv_w,
            "conv_b": conv_b, "w_down": w_down}


def reference(x, norm_mix, w_in, b_forget, lam_re, lam_im, b_re, b_im, c_re, c_im,
              d_skip, log_dt, w_glu, b_glu, q_norm, k_norm, norm_out_ssm,
              norm_out_attn, w_out, norm_ffn, w_up, conv_w, conv_b, w_down):
    bsz, L, _ = x.shape
    h = x
    for l in range(DEPTH):
        hn = rmsnorm(h, norm_mix[l])
        proj = hn @ w_in[l]
        o = 0
        u_ssm = proj[..., o:o + D_SSM]; o += D_SSM
        q = proj[..., o:o + D_ATTN].reshape(bsz, L, N_HEADS, HEAD_DIM); o += D_ATTN
        k = proj[..., o:o + D_ATTN].reshape(bsz, L, N_HEADS, HEAD_DIM); o += D_ATTN
        v = proj[..., o:o + D_ATTN].reshape(bsz, L, N_HEADS, HEAD_DIM); o += D_ATTN
        f_logit = proj[..., o:o + N_HEADS] + b_forget[l]

        y_ssm = s5_mixer(u_ssm, lam_re[l], lam_im[l], b_re[l], b_im[l], c_re[l],
                         c_im[l], d_skip[l], log_dt[l], w_glu[l], b_glu[l]).astype(h.dtype)
        q = rmsnorm(q, q_norm[l])
        k = rmsnorm(k, k_norm[l])
        y_attn = fox_attention(q, k, v, f_logit).reshape(bsz, L, D_ATTN)

        mixed = jnp.concatenate([rmsnorm(y_ssm, norm_out_ssm[l]),
                                 rmsnorm(y_attn, norm_out_attn[l])], axis=-1)
        h = h + mixed @ w_out[l]
        h = h + conv_ffn(rmsnorm(h, norm_ffn[l]), w_up[l], conv_w[l], conv_b[l], w_down[l])
    return h
```

```python
import functools

import jax
import jax.numpy as jnp
from jax import lax
from jax.experimental import pallas as pl
from jax.experimental.pallas import tpu as pltpu

F32 = jnp.float32
BF16 = jnp.bfloat16

D_MODEL = 1024
D_SSM = 512
N_GROUPS = 32
GROUP = 16
STATE = 64
N_STATE = N_GROUPS * STATE
D_ATTN = 512
N_HEADS = 8
HEAD_DIM = 64
D_FF = 2752
EPS = 1e-6

LANES = 128
D_FF_PAD = 2816
W_IN_COLS = D_SSM + 3 * D_ATTN + LANES

ROWS_IN = 512
T_SCAN = 32
TQ = 256
ROWS_FFN = 512
FF_CHUNK = 256

VMEM_LIMIT = 56 * 1024 * 1024


def _const_spec(shape):
    nd = len(shape)
    return pl.BlockSpec(shape, lambda *_: (0,) * nd, pipeline_mode=pl.Buffered(1))


def _rms(x, g):
    return x * lax.rsqrt(jnp.mean(x * x, axis=-1, keepdims=True) + EPS) * g


def _s5_discretize_kernel(lr_ref, li_ref, ldt_ref, br_ref, bi_ref,
                          are_ref, aim_ref, bbr_ref, bbi_ref):
    lr = lr_ref[...]
    li = li_ref[...]
    dt = jnp.exp(ldt_ref[...])
    mag = jnp.exp(lr * dt)
    ab_re = mag * jnp.cos(li * dt)
    ab_im = mag * jnp.sin(li * dt)
    nr = ab_re - 1.0
    ni = ab_im
    den = lr * lr + li * li
    q_re = (nr * lr + ni * li) / den
    q_im = (ni * lr - nr * li) / den
    are_ref[...] = jnp.broadcast_to(ab_re, are_ref.shape)
    aim_ref[...] = jnp.broadcast_to(ab_im, aim_ref.shape)
    br = br_ref[...]
    bi = bi_ref[...]
    bbr_ref[...] = q_re * br - q_im * bi
    bbi_ref[...] = q_re * bi + q_im * br


def _s5_discretize(lam_re, lam_im, log_dt, b_re, b_im):
    lr = lam_re.reshape(1, N_STATE)
    li = lam_im.reshape(1, N_STATE)
    ldt = jnp.repeat(log_dt, STATE).reshape(1, N_STATE)
    br = jnp.transpose(b_re, (2, 0, 1)).reshape(GROUP, N_STATE)
    bi = jnp.transpose(b_im, (2, 0, 1)).reshape(GROUP, N_STATE)
    return pl.pallas_call(
        _s5_discretize_kernel,
        out_shape=(jax.ShapeDtypeStruct((8, N_STATE), F32),
                   jax.ShapeDtypeStruct((8, N_STATE), F32),
                   jax.ShapeDtypeStruct((GROUP, N_STATE), F32),
                   jax.ShapeDtypeStruct((GROUP, N_STATE), F32)),
        name="s5_discretize",
    )(lr, li, ldt, br, bi)


def _in_proj_kernel(x_ref, g_ref, w_ref, bf_ref, qg_ref, kg_ref, ones_ref, tri_ref,
                    u_ref, q_ref, kt_ref, v_ref, crow_ref, carry_ref):
    @pl.when(pl.program_id(1) == 0)
    def _():
        carry_ref[...] = jnp.zeros_like(carry_ref)

    hn = _rms(x_ref[...], g_ref[...]).astype(BF16)
    proj = jnp.dot(hn, w_ref[...], preferred_element_type=F32)
    u_ref[...] = proj[:, :D_SSM]
    q = proj[:, D_SSM:D_SSM + D_ATTN]
    k = proj[:, D_SSM + D_ATTN:D_SSM + 2 * D_ATTN]
    v = proj[:, D_SSM + 2 * D_ATTN:D_SSM + 3 * D_ATTN]
    f = proj[:, D_SSM + 3 * D_ATTN:]

    def head_norm(z, g):
        ss = jnp.dot((z * z).astype(BF16), ones_ref[...], preferred_element_type=F32)
        return z * lax.rsqrt(ss * (1.0 / HEAD_DIM) + EPS) * g

    q_ref[...] = head_norm(q, qg_ref[...]).astype(BF16)
    kt_ref[...] = head_norm(k, kg_ref[...]).T.astype(BF16)
    v_ref[...] = v.astype(BF16)

    z = f + bf_ref[...]
    logf = jnp.minimum(z, 0.0) - jnp.log1p(jnp.exp(-jnp.abs(z)))
    hi = logf.astype(BF16)
    r1 = logf - hi.astype(F32)
    mid = r1.astype(BF16)
    lo = (r1 - mid.astype(F32)).astype(BF16)
    tri = tri_ref[...]
    cum = (jnp.dot(tri, hi, preferred_element_type=F32)
           + jnp.dot(tri, mid, preferred_element_type=F32)
           + jnp.dot(tri, lo, preferred_element_type=F32)) + carry_ref[0:1, :]
    carry_ref[...] = jnp.broadcast_to(cum[ROWS_IN - 1:ROWS_IN, :], carry_ref.shape)
    crow_ref[...] = cum.T[:N_HEADS, :]


def _in_proj(x, norm_g, w_cat, bf_pad, qg, kg, ones_bd, tri):
    bsz, seq, _ = x.shape
    nt = seq // ROWS_IN
    row = lambda b, t: (b, t, 0)
    return pl.pallas_call(
        _in_proj_kernel,
        grid=(bsz, nt),
        in_specs=[
            pl.BlockSpec((None, ROWS_IN, D_MODEL), row),
            _const_spec((1, D_MODEL)),
            _const_spec((D_MODEL, W_IN_COLS)),
            _const_spec((1, LANES)),
            _const_spec((1, D_ATTN)),
            _const_spec((1, D_ATTN)),
            _const_spec((D_ATTN, D_ATTN)),
            _const_spec((ROWS_IN, ROWS_IN)),
        ],
        out_specs=[
            pl.BlockSpec((None, ROWS_IN, D_SSM), row),
            pl.BlockSpec((None, ROWS_IN, D_ATTN), row),
            pl.BlockSpec((None, D_ATTN, ROWS_IN), lambda b, t: (b, 0, t)),
            pl.BlockSpec((None, ROWS_IN, D_ATTN), row),
            pl.BlockSpec((None, N_HEADS, ROWS_IN), lambda b, t: (b, 0, t)),
        ],
        out_shape=[
            jax.ShapeDtypeStruct((bsz, seq, D_SSM), F32),
            jax.ShapeDtypeStruct((bsz, seq, D_ATTN), BF16),
            jax.ShapeDtypeStruct((bsz, D_ATTN, seq), BF16),
            jax.ShapeDtypeStruct((bsz, seq, D_ATTN), BF16),
            jax.ShapeDtypeStruct((bsz, N_HEADS, seq), F32),
        ],
        scratch_shapes=[pltpu.VMEM((8, LANES), F32)],
        compiler_params=pltpu.CompilerParams(
            dimension_semantics=("arbitrary", "arbitrary"), vmem_limit_bytes=VMEM_LIMIT),
        name="in_proj",
    )(x, norm_g, w_cat, bf_pad, qg, kg, ones_bd, tri)


SCAN_LANES = 512


def _s5_scan_kernel(u_ref, are_ref, aim_ref, wb_ref, wc_ref, dskip_ref, wglu_ref, bglu_ref,
                    y_ref, utb_ref, bu_ref, xs_ref, st_ref, ytb_ref):
    bsz = u_ref.shape[0]
    rows = T_SCAN * bsz

    @pl.when(pl.program_id(0) == 0)
    def _():
        st_ref[...] = jnp.zeros_like(st_ref)

    for t in range(T_SCAN):
        utb_ref[t * bsz:(t + 1) * bsz, :] = u_ref[:, t, :]
    u_tb = utb_ref[...]
    ub = u_tb.astype(BF16)

    n_tiles = N_STATE // 256
    for j in range(2 * n_tiles):
        jj = j % n_tiles
        cs = LANES * (jj // 2)
        bu_ref[:, 256 * j:256 * (j + 1)] = jnp.dot(
            ub[:, cs:cs + LANES], wb_ref[j], preferred_element_type=F32)

    half = bsz // 2
    for c in range(N_STATE // SCAN_LANES):
        re = slice(c * SCAN_LANES, (c + 1) * SCAN_LANES)
        im = slice(N_STATE + c * SCAN_LANES, N_STATE + (c + 1) * SCAN_LANES)
        ar = are_ref[:, re]
        ai = aim_ref[:, re]

        def step(t, carry):
            r0 = pl.multiple_of(t * bsz, bsz)
            bur = bu_ref[pl.ds(r0, bsz), re]
            bui = bu_ref[pl.ds(r0, bsz), im]
            new = []
            for s in range(2):
                xr, xi = carry[2 * s], carry[2 * s + 1]
                rs = slice(s * half, (s + 1) * half)
                nxr = ar * xr - ai * xi + bur[rs]
                nxi = ar * xi + ai * xr + bui[rs]
                new += [nxr, nxi]
            xs_ref[pl.ds(r0, bsz), re] = jnp.concatenate([new[0], new[2]], axis=0).astype(BF16)
            xs_ref[pl.ds(r0, bsz), im] = jnp.concatenate([new[1], new[3]], axis=0).astype(BF16)
            return tuple(new)

        init = (st_ref[0:half, re], st_ref[0:half, im], st_ref[half:bsz, re], st_ref[half:bsz, im])
        fin = lax.fori_loop(0, T_SCAN, step, init, unroll=2)
        st_ref[0:half, re] = fin[0]
        st_ref[0:half, im] = fin[1]
        st_ref[half:bsz, re] = fin[2]
        st_ref[half:bsz, im] = fin[3]

    ys = []
    for m in range(D_SSM // LANES):
        xm = jnp.concatenate(
            [xs_ref[:, 512 * m:512 * (m + 1)], xs_ref[:, N_STATE + 512 * m:N_STATE + 512 * (m + 1)]],
            axis=1)
        ys.append(jnp.dot(xm, wc_ref[m], preferred_element_type=F32))
    y = jnp.concatenate(ys, axis=1) + dskip_ref[...] * u_tb
    y = jax.nn.gelu(y)
    z = jnp.dot(y.astype(BF16), wglu_ref[...], preferred_element_type=F32) + bglu_ref[...]
    ytb_ref[...] = y * jax.nn.sigmoid(z)
    for t in range(T_SCAN):
        y_ref[:, t, :] = ytb_ref[t * bsz:(t + 1) * bsz, :]


def _s5_scan(u, a_re, a_im, wb, wc, d_skip, w_glu, b_glu):
    bsz, seq, _ = u.shape
    rows = T_SCAN * bsz
    blk = lambda t: (0, t, 0)
    return pl.pallas_call(
        _s5_scan_kernel,
        grid=(seq // T_SCAN,),
        in_specs=[
            pl.BlockSpec((bsz, T_SCAN, D_SSM), blk),
            _const_spec((8, N_STATE)),
            _const_spec((8, N_STATE)),
            _const_spec((2 * N_STATE // 256, LANES, 256)),
            _const_spec((D_SSM // LANES, 1024, LANES)),
            _const_spec((1, D_SSM)),
            _const_spec((D_SSM, D_SSM)),
            _const_spec((1, D_SSM)),
        ],
        out_specs=pl.BlockSpec((bsz, T_SCAN, D_SSM), blk),
        out_shape=jax.ShapeDtypeStruct((bsz, seq, D_SSM), F32),
        scratch_shapes=[
            pltpu.VMEM((rows, D_SSM), F32),
            pltpu.VMEM((rows, 2 * N_STATE), F32),
            pltpu.VMEM((rows, 2 * N_STATE), BF16),
            pltpu.VMEM((bsz, 2 * N_STATE), F32),
            pltpu.VMEM((rows, D_SSM), F32),
        ],
        compiler_params=pltpu.CompilerParams(
            dimension_semantics=("arbitrary",), vmem_limit_bytes=VMEM_LIMIT),
        name="s5_scan",
    )(u, a_re, a_im, wb, wc, d_skip, w_glu, b_glu)


NEG_BIG = -1e30


def _fox_kernel(q_ref, kt_ref, v_ref, crow_ref, o_ref, m_ref, l_ref, acc_ref):
    seq = q_ref.shape[0]
    nq = seq // TQ
    hp = pl.program_id(1)
    lane = lax.broadcasted_iota(jnp.int32, (1, LANES), 1)
    row_i = lax.broadcasted_iota(jnp.int32, (TQ, TQ), 0)
    col_i = lax.broadcasted_iota(jnp.int32, (TQ, TQ), 1)
    causal = row_i >= col_i

    def tile(a):
        return jnp.concatenate([a] * (TQ // LANES), axis=1)

    def q_block(qi, _):
        q0 = pl.multiple_of(qi * TQ, TQ)
        q2 = q_ref[pl.ds(q0, TQ), :]
        outs = []
        for head in range(2):
            hmask = (lane < HEAD_DIM) if head == 0 else (lane >= HEAD_DIM)
            qh = jnp.where(hmask, q2, jnp.zeros_like(q2))
            hrow = 2 * hp + head
            c0 = jnp.min(crow_ref[pl.ds(hrow, 1), pl.ds(q0, TQ)], axis=1, keepdims=True)
            m_ref[...] = jnp.full_like(m_ref, NEG_BIG)
            l_ref[...] = jnp.zeros_like(l_ref)
            acc_ref[...] = jnp.zeros_like(acc_ref)

            def kv_step(j, masked):
                k0 = pl.multiple_of(j * TQ, TQ)
                s = jnp.dot(qh, kt_ref[:, pl.ds(k0, TQ)], preferred_element_type=F32)
                s = s + (c0 - crow_ref[pl.ds(hrow, 1), pl.ds(k0, TQ)])
                if masked:
                    s = jnp.where(causal, s, NEG_BIG)
                m_prev = m_ref[...]
                m_new = jnp.maximum(m_prev, jnp.max(s, axis=1, keepdims=True))
                alpha = jnp.exp(m_prev - m_new)
                p = jnp.exp(s - tile(m_new))
                l_ref[...] = alpha * l_ref[...] + jnp.sum(p, axis=1, keepdims=True)
                acc_ref[...] = alpha * acc_ref[...] + jnp.dot(
                    p.astype(BF16), v_ref[pl.ds(k0, TQ), :], preferred_element_type=F32)
                m_ref[...] = m_new

            def body(j, carry):
                kv_step(j, False)
                return carry

            lax.fori_loop(0, qi, body, 0)
            kv_step(qi, True)
            outs.append(acc_ref[...] / l_ref[...])
        o_ref[pl.ds(q0, TQ), :] = jnp.where(lane < HEAD_DIM, outs[0], outs[1]).astype(o_ref.dtype)
        return 0

    lax.fori_loop(0, nq, q_block, 0)


def _fox_attention(q, kt, v, crow):
    bsz, seq, _ = q.shape
    n_pairs = D_ATTN // LANES
    return pl.pallas_call(
        _fox_kernel,
        grid=(bsz, n_pairs),
        in_specs=[
            pl.BlockSpec((None, seq, LANES), lambda b, h: (b, 0, h)),
            pl.BlockSpec((None, LANES, seq), lambda b, h: (b, h, 0)),
            pl.BlockSpec((None, seq, LANES), lambda b, h: (b, 0, h)),
            pl.BlockSpec((None, N_HEADS, seq), lambda b, h: (b, 0, 0)),
        ],
        out_specs=pl.BlockSpec((None, seq, LANES), lambda b, h: (b, 0, h)),
        out_shape=jax.ShapeDtypeStruct((bsz, seq, D_ATTN), BF16),
        scratch_shapes=[
            pltpu.VMEM((TQ, LANES), F32),
            pltpu.VMEM((TQ, LANES), F32),
            pltpu.VMEM((TQ, LANES), F32),
        ],
        compiler_params=pltpu.CompilerParams(
            dimension_semantics=("arbitrary", "arbitrary"), vmem_limit_bytes=VMEM_LIMIT),
        name="fox_attention",
    )(q, kt, v, crow)


def _out_ffn_kernel(x_ref, ys_ref, ya_ref, gs_ref, ga_ref, wout_ref, gf_ref, wg_ref, wv_ref,
                    cw_ref, cb_ref, wd_ref, o_ref, act_ref, prev_ref):
    @pl.when(pl.program_id(1) == 0)
    def _():
        prev_ref[...] = jnp.zeros_like(prev_ref)

    ms = _rms(ys_ref[...], gs_ref[...])
    ma = _rms(ya_ref[...].astype(F32), ga_ref[...])
    mixed = jnp.concatenate([ms, ma], axis=1).astype(BF16)
    h = x_ref[...] + jnp.dot(mixed, wout_ref[...], preferred_element_type=F32)
    hn = _rms(h, gf_ref[...]).astype(BF16)

    rows = ROWS_FFN

    def conv(u, off):
        ext = jnp.concatenate([prev_ref[:, off:off + FF_CHUNK], u], axis=0)
        prev_ref[:, off:off + FF_CHUNK] = u[rows - 8:, :]
        w = cw_ref[:, off:off + FF_CHUNK]
        return (ext[6:rows + 6] * w[0:1] + ext[7:rows + 7] * w[1:2] + ext[8:] * w[2:3]
                + cb_ref[:, off:off + FF_CHUNK])

    for c in range(D_FF_PAD // FF_CHUNK):
        lo = c * FF_CHUNK
        ug = jnp.dot(hn, wg_ref[:, lo:lo + FF_CHUNK], preferred_element_type=F32)
        uv = jnp.dot(hn, wv_ref[:, lo:lo + FF_CHUNK], preferred_element_type=F32)
        cg = conv(ug, lo)
        cv = conv(uv, D_FF_PAD + lo)
        act_ref[:, lo:lo + FF_CHUNK] = (cg * jax.nn.sigmoid(cg) * cv).astype(BF16)
    o_ref[...] = h + jnp.dot(act_ref[...], wd_ref[...], preferred_element_type=F32)


def _out_ffn(x, y_ssm, y_attn, gs, ga, w_out, gf, wg, wv, cw, cb, wd):
    bsz, seq, _ = x.shape
    row = lambda b, t: (b, t, 0)
    return pl.pallas_call(
        _out_ffn_kernel,
        grid=(bsz, seq // ROWS_FFN),
        in_specs=[
            pl.BlockSpec((None, ROWS_FFN, D_MODEL), row),
            pl.BlockSpec((None, ROWS_FFN, D_SSM), row),
            pl.BlockSpec((None, ROWS_FFN, D_ATTN), row),
            _const_spec((1, D_SSM)),
            _const_spec((1, D_ATTN)),
            _const_spec((D_MODEL, D_MODEL)),
            _const_spec((1, D_MODEL)),
            _const_spec((D_MODEL, D_FF_PAD)),
            _const_spec((D_MODEL, D_FF_PAD)),
            _const_spec((3, 2 * D_FF_PAD)),
            _const_spec((1, 2 * D_FF_PAD)),
            _const_spec((D_FF_PAD, D_MODEL)),
        ],
        out_specs=pl.BlockSpec((None, ROWS_FFN, D_MODEL), row),
        out_shape=jax.ShapeDtypeStruct((bsz, seq, D_MODEL), F32),
        scratch_shapes=[
            pltpu.VMEM((ROWS_FFN, D_FF_PAD), BF16),
            pltpu.VMEM((8, 2 * D_FF_PAD), F32),
        ],
        compiler_params=pltpu.CompilerParams(
            dimension_semantics=("arbitrary", "arbitrary"), vmem_limit_bytes=VMEM_LIMIT),
        name="out_ffn",
    )(x, y_ssm, y_attn, gs, ga, w_out, gf, wg, wv, cw, cb, wd)


def _pad_cols(a, n):
    return jnp.pad(a, ((0, 0), (0, n - a.shape[1])))


def _layer(h, norm_mix, w_in, b_forget, lam_re, lam_im, b_re, b_im, c_re, c_im, d_skip, log_dt,
           w_glu, b_glu, q_norm, k_norm, norm_out_ssm, norm_out_attn, w_out, norm_ffn, w_up,
           conv_w, conv_b, w_down):
    G, P, H = N_GROUPS, STATE, GROUP
    a_re, a_im, bbr, bbi = _s5_discretize(lam_re, lam_im, log_dt, b_re, b_im)
    eye = jnp.eye(G, dtype=F32)

    def b_tiles(bb):
        full = (eye[:, None, :, None] * bb.reshape(H, G, P).transpose(1, 0, 2)[:, :, None, :])
        full = full.reshape(D_SSM // LANES, LANES, N_STATE // 256, 256)
        j = jnp.arange(N_STATE // 256)
        return full[j // 2, :, j, :]

    wb = jnp.concatenate([b_tiles(bbr), b_tiles(bbi)], axis=0).astype(BF16)

    def c_full(cc):
        return (eye[:, None, :, None] * cc.transpose(0, 2, 1)[:, :, None, :]).reshape(N_STATE, D_SSM)

    cr, ci = c_full(c_re), c_full(c_im)
    wc = jnp.stack([
        jnp.concatenate([cr[512 * m:512 * (m + 1), LANES * m:LANES * (m + 1)],
                         -ci[512 * m:512 * (m + 1), LANES * m:LANES * (m + 1)]], axis=0)
        for m in range(D_SSM // LANES)]).astype(BF16)

    w_cat = _pad_cols(w_in, W_IN_COLS).astype(BF16)
    bf_pad = _pad_cols(b_forget.reshape(1, N_HEADS), LANES)
    qg = jnp.tile(q_norm * (HEAD_DIM ** -0.5), N_HEADS).reshape(1, D_ATTN)
    kg = jnp.tile(k_norm, N_HEADS).reshape(1, D_ATTN)
    head_id = jnp.arange(D_ATTN) // HEAD_DIM
    ones_bd = (head_id[:, None] == head_id[None, :]).astype(BF16)
    r = jnp.arange(ROWS_IN)
    tri = (r[:, None] >= r[None, :]).astype(BF16)

    u, q, kt, v, crow = _in_proj(h, norm_mix.reshape(1, D_MODEL), w_cat, bf_pad, qg, kg, ones_bd, tri)
    y_ssm = _s5_scan(u, a_re, a_im, wb, wc, d_skip.reshape(1, D_SSM), w_glu.astype(BF16),
                     b_glu.reshape(1, D_SSM))
    y_attn = _fox_attention(q, kt, v, crow)

    wg = _pad_cols(w_up[:, :D_FF], D_FF_PAD).astype(BF16)
    wv = _pad_cols(w_up[:, D_FF:], D_FF_PAD).astype(BF16)
    cw = jnp.concatenate([_pad_cols(conv_w[:, :D_FF], D_FF_PAD), _pad_cols(conv_w[:, D_FF:], D_FF_PAD)], axis=1)
    cbp = jnp.concatenate([_pad_cols(conv_b[None, :D_FF], D_FF_PAD), _pad_cols(conv_b[None, D_FF:], D_FF_PAD)], axis=1)
    wd = jnp.pad(w_down, ((0, D_FF_PAD - D_FF), (0, 0))).astype(BF16)
    return _out_ffn(h, y_ssm, y_attn, norm_out_ssm.reshape(1, D_SSM), norm_out_attn.reshape(1, D_ATTN),
                    w_out.astype(BF16), norm_ffn.reshape(1, D_MODEL), wg, wv, cw, cbp, wd)


def kernel(x, norm_mix, w_in, b_forget, lam_re, lam_im, b_re, b_im, c_re, c_im, d_skip, log_dt,
           w_glu, b_glu, q_norm, k_norm, norm_out_ssm, norm_out_attn, w_out, norm_ffn, w_up, conv_w,
           conv_b, w_down):
    h = x
    for l in range(norm_mix.shape[0]):
        h = _layer(h, norm_mix[l], w_in[l], b_forget[l], lam_re[l], lam_im[l], b_re[l], b_im[l],
                   c_re[l], c_im[l], d_skip[l], log_dt[l], w_glu[l], b_glu[l], q_norm[l], k_norm[l],
                   norm_out_ssm[l], norm_out_attn[l], w_out[l], norm_ffn[l], w_up[l], conv_w[l],
                   conv_b[l], w_down[l])
    return h
```

```python
import functools

import jax
import jax.numpy as jnp
from jax import lax
from jax.experimental import pallas as pl
from jax.experimental.pallas import tpu as pltpu

F32 = jnp.float32
BF16 = jnp.bfloat16

D_MODEL = 1024
D_SSM = 512
N_GROUPS = 32
GROUP = 16
STATE = 64
N_STATE = N_GROUPS * STATE
D_ATTN = 512
N_HEADS = 8
HEAD_DIM = 64
D_FF = 2752
EPS = 1e-6

LANES = 128
D_FF_PAD = 2816
W_IN_COLS = D_SSM + 3 * D_ATTN + LANES

ROWS_IN = 512
T_SCAN = 32
TQ = 512
ROWS_FFN = 512
FF_CHUNK = 256

VMEM_LIMIT = 56 * 1024 * 1024


def _const_spec(shape):
    nd = len(shape)
    return pl.BlockSpec(shape, lambda *_: (0,) * nd, pipeline_mode=pl.Buffered(1))


def _rms(x, g):
    return x * lax.rsqrt(jnp.mean(x * x, axis=-1, keepdims=True) + EPS) * g


def _s5_discretize_kernel(lr_ref, li_ref, ldt_ref, br_ref, bi_ref,
                          are_ref, aim_ref, bbr_ref, bbi_ref):
    lr = lr_ref[...]
    li = li_ref[...]
    dt = jnp.exp(ldt_ref[...])
    mag = jnp.exp(lr * dt)
    ab_re = mag * jnp.cos(li * dt)
    ab_im = mag * jnp.sin(li * dt)
    nr = ab_re - 1.0
    ni = ab_im
    den = lr * lr + li * li
    q_re = (nr * lr + ni * li) / den
    q_im = (ni * lr - nr * li) / den
    are_ref[...] = jnp.broadcast_to(ab_re, are_ref.shape)
    aim_ref[...] = jnp.broadcast_to(ab_im, aim_ref.shape)
    br = br_ref[...]
    bi = bi_ref[...]
    bbr_ref[...] = q_re * br - q_im * bi
    bbi_ref[...] = q_re * bi + q_im * br


def _s5_discretize(lam_re, lam_im, log_dt, b_re, b_im):
    lr = lam_re.reshape(1, N_STATE)
    li = lam_im.reshape(1, N_STATE)
    ldt = jnp.repeat(log_dt, STATE).reshape(1, N_STATE)
    br = jnp.transpose(b_re, (2, 0, 1)).reshape(GROUP, N_STATE)
    bi = jnp.transpose(b_im, (2, 0, 1)).reshape(GROUP, N_STATE)
    return pl.pallas_call(
        _s5_discretize_kernel,
        out_shape=(jax.ShapeDtypeStruct((8, N_STATE), F32),
                   jax.ShapeDtypeStruct((8, N_STATE), F32),
                   jax.ShapeDtypeStruct((GROUP, N_STATE), F32),
                   jax.ShapeDtypeStruct((GROUP, N_STATE), F32)),
        name="s5_discretize",
    )(lr, li, ldt, br, bi)


def _in_proj_kernel(x_ref, g_ref, w_ref, bf_ref, qg_ref, kg_ref, ones_ref, tri_ref,
                    u_ref, q_ref, kt_ref, v_ref, crow_ref, carry_ref):
    @pl.when(pl.program_id(1) == 0)
    def _():
        carry_ref[...] = jnp.zeros_like(carry_ref)

    hn = _rms(x_ref[...], g_ref[...]).astype(BF16)
    proj = jnp.dot(hn, w_ref[...], preferred_element_type=F32)
    u_ref[...] = proj[:, :D_SSM]
    q = proj[:, D_SSM:D_SSM + D_ATTN]
    k = proj[:, D_SSM + D_ATTN:D_SSM + 2 * D_ATTN]
    v = proj[:, D_SSM + 2 * D_ATTN:D_SSM + 3 * D_ATTN]
    f = proj[:, D_SSM + 3 * D_ATTN:]

    def head_norm(z, g):
        ss = jnp.dot((z * z).astype(BF16), ones_ref[...], preferred_element_type=F32)
        return z * lax.rsqrt(ss * (1.0 / HEAD_DIM) + EPS) * g

    q_ref[...] = head_norm(q, qg_ref[...]).astype(BF16)
    kt_ref[...] = head_norm(k, kg_ref[...]).T.astype(BF16)
    v_ref[...] = v.astype(BF16)

    z = f + bf_ref[...]
    logf = jnp.minimum(z, 0.0) - jnp.log1p(jnp.exp(-jnp.abs(z)))
    hi = logf.astype(BF16)
    r1 = logf - hi.astype(F32)
    mid = r1.astype(BF16)
    lo = (r1 - mid.astype(F32)).astype(BF16)
    tri = tri_ref[...]
    cum = (jnp.dot(tri, hi, preferred_element_type=F32)
           + jnp.dot(tri, mid, preferred_element_type=F32)
           + jnp.dot(tri, lo, preferred_element_type=F32)) + carry_ref[0:1, :]
    carry_ref[...] = jnp.broadcast_to(cum[ROWS_IN - 1:ROWS_IN, :], carry_ref.shape)
    crow_ref[...] = cum.T[:N_HEADS, :]


def _in_proj(x, norm_g, w_cat, bf_pad, qg, kg, ones_bd, tri):
    bsz, seq, _ = x.shape
    nt = seq // ROWS_IN
    row = lambda b, t: (b, t, 0)
    return pl.pallas_call(
        _in_proj_kernel,
        grid=(bsz, nt),
        in_specs=[
            pl.BlockSpec((None, ROWS_IN, D_MODEL), row),
            _const_spec((1, D_MODEL)),
            _const_spec((D_MODEL, W_IN_COLS)),
            _const_spec((1, LANES)),
            _const_spec((1, D_ATTN)),
            _const_spec((1, D_ATTN)),
            _const_spec((D_ATTN, D_ATTN)),
            _const_spec((ROWS_IN, ROWS_IN)),
        ],
        out_specs=[
            pl.BlockSpec((None, ROWS_IN, D_SSM), row),
            pl.BlockSpec((None, ROWS_IN, D_ATTN), row),
            pl.BlockSpec((None, D_ATTN, ROWS_IN), lambda b, t: (b, 0, t)),
            pl.BlockSpec((None, ROWS_IN, D_ATTN), row),
            pl.BlockSpec((None, N_HEADS, ROWS_IN), lambda b, t: (b, 0, t)),
        ],
        out_shape=[
            jax.ShapeDtypeStruct((bsz, seq, D_SSM), F32),
            jax.ShapeDtypeStruct((bsz, seq, D_ATTN), BF16),
            jax.ShapeDtypeStruct((bsz, D_ATTN, seq), BF16),
            jax.ShapeDtypeStruct((bsz, seq, D_ATTN), BF16),
            jax.ShapeDtypeStruct((bsz, N_HEADS, seq), F32),
        ],
        scratch_shapes=[pltpu.VMEM((8, LANES), F32)],
        compiler_params=pltpu.CompilerParams(
            dimension_semantics=("arbitrary", "arbitrary"), vmem_limit_bytes=VMEM_LIMIT),
        name="in_proj",
    )(x, norm_g, w_cat, bf_pad, qg, kg, ones_bd, tri)


SCAN_LANES = 512


def _s5_scan_kernel(u_ref, are_ref, aim_ref, wb_ref, wc_ref, dskip_ref, wglu_ref, bglu_ref,
                    y_ref, utb_ref, bu_ref, xs_ref, st_ref, ytb_ref):
    bsz = u_ref.shape[0]
    rows = T_SCAN * bsz

    @pl.when(pl.program_id(0) == 0)
    def _():
        st_ref[...] = jnp.zeros_like(st_ref)

    for t in range(T_SCAN):
        utb_ref[t * bsz:(t + 1) * bsz, :] = u_ref[:, t, :]
    u_tb = utb_ref[...]
    ub = u_tb.astype(BF16)

    n_tiles = N_STATE // 256
    for j in range(2 * n_tiles):
        jj = j % n_tiles
        cs = LANES * (jj // 2)
        bu_ref[:, 256 * j:256 * (j + 1)] = jnp.dot(
            ub[:, cs:cs + LANES], wb_ref[j], preferred_element_type=F32)

    half = bsz // 2
    for c in range(N_STATE // SCAN_LANES):
        re = slice(c * SCAN_LANES, (c + 1) * SCAN_LANES)
        im = slice(N_STATE + c * SCAN_LANES, N_STATE + (c + 1) * SCAN_LANES)
        ar = are_ref[:, re]
        ai = aim_ref[:, re]

        def step(t, carry):
            r0 = pl.multiple_of(t * bsz, bsz)
            bur = bu_ref[pl.ds(r0, bsz), re]
            bui = bu_ref[pl.ds(r0, bsz), im]
            new = []
            for s in range(2):
                xr, xi = carry[2 * s], carry[2 * s + 1]
                rs = slice(s * half, (s + 1) * half)
                nxr = ar * xr - ai * xi + bur[rs]
                nxi = ar * xi + ai * xr + bui[rs]
                new += [nxr, nxi]
            xs_ref[pl.ds(r0, bsz), re] = jnp.concatenate([new[0], new[2]], axis=0).astype(BF16)
            xs_ref[pl.ds(r0, bsz), im] = jnp.concatenate([new[1], new[3]], axis=0).astype(BF16)
            return tuple(new)

        init = (st_ref[0:half, re], st_ref[0:half, im], st_ref[half:bsz, re], st_ref[half:bsz, im])
        fin = lax.fori_loop(0, T_SCAN, step, init, unroll=2)
        st_ref[0:half, re] = fin[0]
        st_ref[0:half, im] = fin[1]
        st_ref[half:bsz, re] = fin[2]
        st_ref[half:bsz, im] = fin[3]

    ys = []
    for m in range(D_SSM // LANES):
        xm = jnp.concatenate(
            [xs_ref[:, 512 * m:512 * (m + 1)], xs_ref[:, N_STATE + 512 * m:N_STATE + 512 * (m + 1)]],
            axis=1)
        ys.append(jnp.dot(xm, wc_ref[m], preferred_element_type=F32))
    y = jnp.concatenate(ys, axis=1) + dskip_ref[...] * u_tb
    y = jax.nn.gelu(y)
    z = jnp.dot(y.astype(BF16), wglu_ref[...], preferred_element_type=F32) + bglu_ref[...]
    ytb_ref[...] = y * jax.nn.sigmoid(z)
    for t in range(T_SCAN):
        y_ref[:, t, :] = ytb_ref[t * bsz:(t + 1) * bsz, :]


def _s5_scan(u, a_re, a_im, wb, wc, d_skip, w_glu, b_glu):
    bsz, seq, _ = u.shape
    rows = T_SCAN * bsz
    blk = lambda t: (0, t, 0)
    return pl.pallas_call(
        _s5_scan_kernel,
        grid=(seq // T_SCAN,),
        in_specs=[
            pl.BlockSpec((bsz, T_SCAN, D_SSM), blk),
            _const_spec((8, N_STATE)),
            _const_spec((8, N_STATE)),
            _const_spec((2 * N_STATE // 256, LANES, 256)),
            _const_spec((D_SSM // LANES, 1024, LANES)),
            _const_spec((1, D_SSM)),
            _const_spec((D_SSM, D_SSM)),
            _const_spec((1, D_SSM)),
        ],
        out_specs=pl.BlockSpec((bsz, T_SCAN, D_SSM), blk),
        out_shape=jax.ShapeDtypeStruct((bsz, seq, D_SSM), F32),
        scratch_shapes=[
            pltpu.VMEM((rows, D_SSM), F32),
            pltpu.VMEM((rows, 2 * N_STATE), F32),
            pltpu.VMEM((rows, 2 * N_STATE), BF16),
            pltpu.VMEM((bsz, 2 * N_STATE), F32),
            pltpu.VMEM((rows, D_SSM), F32),
        ],
        compiler_params=pltpu.CompilerParams(
            dimension_semantics=("arbitrary",), vmem_limit_bytes=VMEM_LIMIT),
        name="s5_scan",
    )(u, a_re, a_im, wb, wc, d_skip, w_glu, b_glu)


NEG_BIG = -1e30
SUB = TQ // 2


def _fox_kernel(q_ref, kt_ref, v_ref, crow_ref, o_ref, qh_ref, m_ref, l_ref, acc_ref):
    seq = q_ref.shape[0]
    nq = seq // TQ
    hp = pl.program_id(1)
    lane = lax.broadcasted_iota(jnp.int32, (1, LANES), 1)

    def q_block(qi, _):
        q0 = pl.multiple_of(qi * TQ, TQ)
        q2 = q_ref[pl.ds(q0, TQ), :]
        c0 = []
        for head in range(2):
            hmask = (lane < HEAD_DIM) if head == 0 else (lane >= HEAD_DIM)
            qh_ref[head] = jnp.where(hmask, q2, jnp.zeros_like(q2))
            c0.append(jnp.min(crow_ref[pl.ds(2 * hp + head, 1), pl.ds(q0, TQ)], axis=1, keepdims=True))
        m_ref[...] = jnp.full_like(m_ref, NEG_BIG)
        l_ref[...] = jnp.zeros_like(l_ref)
        acc_ref[...] = jnp.zeros_like(acc_ref)

        def update(head, r0, nrows, k0, klen, masked):
            rows = slice(r0, r0 + nrows)
            s = jnp.dot(qh_ref[head, rows], kt_ref[:, pl.ds(k0, klen)], preferred_element_type=F32)
            s = s + (c0[head] - crow_ref[pl.ds(2 * hp + head, 1), pl.ds(k0, klen)])
            if masked:
                row_i = lax.broadcasted_iota(jnp.int32, (nrows, klen), 0) + r0
                col_i = lax.broadcasted_iota(jnp.int32, (nrows, klen), 1)
                s = jnp.where(row_i >= col_i, s, NEG_BIG)
            m_prev = m_ref[head, rows]
            m_new = jnp.maximum(m_prev, jnp.max(s, axis=1, keepdims=True))
            alpha = jnp.exp(m_prev - m_new)
            p = jnp.exp(s - jnp.concatenate([m_new] * (klen // LANES), axis=1))
            l_ref[head, rows] = alpha * l_ref[head, rows] + jnp.sum(p, axis=1, keepdims=True)
            acc_ref[head, rows] = alpha * acc_ref[head, rows] + jnp.dot(
                p.astype(BF16), v_ref[pl.ds(k0, klen), :], preferred_element_type=F32)
            m_ref[head, rows] = m_new

        def body(j, carry):
            k0 = pl.multiple_of(j * TQ, TQ)
            for head in range(2):
                update(head, 0, TQ, k0, TQ, False)
            return carry

        lax.fori_loop(0, qi, body, 0)
        for head in range(2):
            update(head, 0, SUB, q0, SUB, True)
            update(head, SUB, SUB, q0, TQ, True)
        o0 = acc_ref[0] / l_ref[0]
        o1 = acc_ref[1] / l_ref[1]
        o_ref[pl.ds(q0, TQ), :] = jnp.where(lane < HEAD_DIM, o0, o1).astype(o_ref.dtype)
        return 0

    lax.fori_loop(0, nq, q_block, 0)


def _fox_attention(q, kt, v, crow):
    bsz, seq, _ = q.shape
    n_pairs = D_ATTN // LANES
    return pl.pallas_call(
        _fox_kernel,
        grid=(bsz, n_pairs),
        in_specs=[
            pl.BlockSpec((None, seq, LANES), lambda b, h: (b, 0, h)),
            pl.BlockSpec((None, LANES, seq), lambda b, h: (b, h, 0)),
            pl.BlockSpec((None, seq, LANES), lambda b, h: (b, 0, h)),
            pl.BlockSpec((None, N_HEADS, seq), lambda b, h: (b, 0, 0)),
        ],
        out_specs=pl.BlockSpec((None, seq, LANES), lambda b, h: (b, 0, h)),
        out_shape=jax.ShapeDtypeStruct((bsz, seq, D_ATTN), BF16),
        scratch_shapes=[
            pltpu.VMEM((2, TQ, LANES), BF16),
            pltpu.VMEM((2, TQ, LANES), F32),
            pltpu.VMEM((2, TQ, LANES), F32),
            pltpu.VMEM((2, TQ, LANES), F32),
        ],
        compiler_params=pltpu.CompilerParams(
            dimension_semantics=("arbitrary", "arbitrary"), vmem_limit_bytes=VMEM_LIMIT),
        name="fox_attention",
    )(q, kt, v, crow)


def _out_ffn_kernel(x_ref, ys_ref, ya_ref, gs_ref, ga_ref, wout_ref, gf_ref, wg_ref, wv_ref,
                    cw_ref, cb_ref, wd_ref, o_ref, act_ref, prev_ref):
    @pl.when(pl.program_id(1) == 0)
    def _():
        prev_ref[...] = jnp.zeros_like(prev_ref)

    ms = _rms(ys_ref[...], gs_ref[...])
    ma = _rms(ya_ref[...].astype(F32), ga_ref[...])
    mixed = jnp.concatenate([ms, ma], axis=1).astype(BF16)
    h = x_ref[...] + jnp.dot(mixed, wout_ref[...], preferred_element_type=F32)
    hn = _rms(h, gf_ref[...]).astype(BF16)

    rows = ROWS_FFN

    def conv(u, off):
        ext = jnp.concatenate([prev_ref[:, off:off + FF_CHUNK], u], axis=0)
        prev_ref[:, off:off + FF_CHUNK] = u[rows - 8:, :]
        w = cw_ref[:, off:off + FF_CHUNK]
        return (ext[6:rows + 6] * w[0:1] + ext[7:rows + 7] * w[1:2] + ext[8:] * w[2:3]
                + cb_ref[:, off:off + FF_CHUNK])

    for c in range(D_FF_PAD // FF_CHUNK):
        lo = c * FF_CHUNK
        ug = jnp.dot(hn, wg_ref[:, lo:lo + FF_CHUNK], preferred_element_type=F32)
        uv = jnp.dot(hn, wv_ref[:, lo:lo + FF_CHUNK], preferred_element_type=F32)
        cg = conv(ug, lo)
        cv = conv(uv, D_FF_PAD + lo)
        act_ref[:, lo:lo + FF_CHUNK] = (cg * jax.nn.sigmoid(cg) * cv).astype(BF16)
    o_ref[...] = h + jnp.dot(act_ref[...], wd_ref[...], preferred_element_type=F32)


def _out_ffn(x, y_ssm, y_attn, gs, ga, w_out, gf, wg, wv, cw, cb, wd):
    bsz, seq, _ = x.shape
    row = lambda b, t: (b, t, 0)
    return pl.pallas_call(
        _out_ffn_kernel,
        grid=(bsz, seq // ROWS_FFN),
        in_specs=[
            pl.BlockSpec((None, ROWS_FFN, D_MODEL), row),
            pl.BlockSpec((None, ROWS_FFN, D_SSM), row),
            pl.BlockSpec((None, ROWS_FFN, D_ATTN), row),
            _const_spec((1, D_SSM)),
            _const_spec((1, D_ATTN)),
            _const_spec((D_MODEL, D_MODEL)),
            _const_spec((1, D_MODEL)),
            _const_spec((D_MODEL, D_FF_PAD)),
            _const_spec((D_MODEL, D_FF_PAD)),
            _const_spec((3, 2 * D_FF_PAD)),
            _const_spec((1, 2 * D_FF_PAD)),
            _const_spec((D_FF_PAD, D_MODEL)),
        ],
        out_specs=pl.BlockSpec((None, ROWS_FFN, D_MODEL), row),
        out_shape=jax.ShapeDtypeStruct((bsz, seq, D_MODEL), F32),
        scratch_shapes=[
            pltpu.VMEM((ROWS_FFN, D_FF_PAD), BF16),
            pltpu.VMEM((8, 2 * D_FF_PAD), F32),
        ],
        compiler_params=pltpu.CompilerParams(
            dimension_semantics=("arbitrary", "arbitrary"), vmem_limit_bytes=VMEM_LIMIT),
        name="out_ffn",
    )(x, y_ssm, y_attn, gs, ga, w_out, gf, wg, wv, cw, cb, wd)


def _pad_cols(a, n):
    return jnp.pad(a, ((0, 0), (0, n - a.shape[1])))


def _layer(h, norm_mix, w_in, b_forget, lam_re, lam_im, b_re, b_im, c_re, c_im, d_skip, log_dt,
           w_glu, b_glu, q_norm, k_norm, norm_out_ssm, norm_out_attn, w_out, norm_ffn, w_up,
           conv_w, conv_b, w_down):
    G, P, H = N_GROUPS, STATE, GROUP
    a_re, a_im, bbr, bbi = _s5_discretize(lam_re, lam_im, log_dt, b_re, b_im)
    eye = jnp.eye(G, dtype=F32)

    def b_tiles(bb):
        full = (eye[:, None, :, None] * bb.reshape(H, G, P).transpose(1, 0, 2)[:, :, None, :])
        full = full.reshape(D_SSM // LANES, LANES, N_STATE // 256, 256)
        j = jnp.arange(N_STATE // 256)
        return full[j // 2, :, j, :]

    wb = jnp.concatenate([b_tiles(bbr), b_tiles(bbi)], axis=0).astype(BF16)

    def c_full(cc):
        return (eye[:, None, :, None] * cc.transpose(0, 2, 1)[:, :, None, :]).reshape(N_STATE, D_SSM)

    cr, ci = c_full(c_re), c_full(c_im)
    wc = jnp.stack([
        jnp.concatenate([cr[512 * m:512 * (m + 1), LANES * m:LANES * (m + 1)],
                         -ci[512 * m:512 * (m + 1), LANES * m:LANES * (m + 1)]], axis=0)
        for m in range(D_SSM // LANES)]).astype(BF16)

    w_cat = _pad_cols(w_in, W_IN_COLS).astype(BF16)
    bf_pad = _pad_cols(b_forget.reshape(1, N_HEADS), LANES)
    qg = jnp.tile(q_norm * (HEAD_DIM ** -0.5), N_HEADS).reshape(1, D_ATTN)
    kg = jnp.tile(k_norm, N_HEADS).reshape(1, D_ATTN)
    head_id = jnp.arange(D_ATTN) // HEAD_DIM
    ones_bd = (head_id[:, None] == head_id[None, :]).astype(BF16)
    r = jnp.arange(ROWS_IN)
    tri = (r[:, None] >= r[None, :]).astype(BF16)

    u, q, kt, v, crow = _in_proj(h, norm_mix.reshape(1, D_MODEL), w_cat, bf_pad, qg, kg, ones_bd, tri)
    y_ssm = _s5_scan(u, a_re, a_im, wb, wc, d_skip.reshape(1, D_SSM), w_glu.astype(BF16),
                     b_glu.reshape(1, D_SSM))
    y_attn = _fox_attention(q, kt, v, crow)

    wg = _pad_cols(w_up[:, :D_FF], D_FF_PAD).astype(BF16)
    wv = _pad_cols(w_up[:, D_FF:], D_FF_PAD).astype(BF16)
    cw = jnp.concatenate([_pad_cols(conv_w[:, :D_FF], D_FF_PAD), _pad_cols(conv_w[:, D_FF:], D_FF_PAD)], axis=1)
    cbp = jnp.concatenate([_pad_cols(conv_b[None, :D_FF], D_FF_PAD), _pad_cols(conv_b[None, D_FF:], D_FF_PAD)], axis=1)
    wd = jnp.pad(w_down, ((0, D_FF_PAD - D_FF), (0, 0))).astype(BF16)
    return _out_ffn(h, y_ssm, y_attn, norm_out_ssm.reshape(1, D_SSM), norm_out_attn.reshape(1, D_ATTN),
                    w_out.astype(BF16), norm_ffn.reshape(1, D_MODEL), wg, wv, cw, cbp, wd)


def kernel(x, norm_mix, w_in, b_forget, lam_re, lam_im, b_re, b_im, c_re, c_im, d_skip, log_dt,
           w_glu, b_glu, q_norm, k_norm, norm_out_ssm, norm_out_attn, w_out, norm_ffn, w_up, conv_w,
           conv_b, w_down):
    h = x
    for l in range(norm_mix.shape[0]):
        h = _layer(h, norm_mix[l], w_in[l], b_forget[l], lam_re[l], lam_im[l], b_re[l], b_im[l],
                   c_re[l], c_im[l], d_skip[l], log_dt[l], w_glu[l], b_glu[l], q_norm[l], k_norm[l],
                   norm_out_ssm[l], norm_out_attn[l], w_out[l], norm_ffn[l], w_up[l], conv_w[l],
                   conv_b[l], w_down[l])
    return h
```

```python
import functools

import jax
import jax.numpy as jnp
from jax import lax
from jax.experimental import pallas as pl
from jax.experimental.pallas import tpu as pltpu

F32 = jnp.float32
BF16 = jnp.bfloat16

D_MODEL = 1024
D_SSM = 512
N_GROUPS = 32
GROUP = 16
STATE = 64
N_STATE = N_GROUPS * STATE
D_ATTN = 512
N_HEADS = 8
HEAD_DIM = 64
D_FF = 2752
EPS = 1e-6
LOG2E = 1.4426950408889634

LANES = 128
D_FF_PAD = 2816
W_IN_COLS = D_SSM + 3 * D_ATTN + LANES

ROWS_IN = 512
T_SCAN = 32
TQ = 256
ROWS_FFN = 512
FF_CHUNK = 256

VMEM_LIMIT = 56 * 1024 * 1024


def _const_spec(shape):
    nd = len(shape)
    return pl.BlockSpec(shape, lambda *_: (0,) * nd, pipeline_mode=pl.Buffered(1))


def _rms(x, g):
    return x * lax.rsqrt(jnp.mean(x * x, axis=-1, keepdims=True) + EPS) * g


def _s5_discretize_kernel(lr_ref, li_ref, ldt_ref, br_ref, bi_ref,
                          are_ref, aim_ref, bbr_ref, bbi_ref):
    lr = lr_ref[...]
    li = li_ref[...]
    dt = jnp.exp(ldt_ref[...])
    mag = jnp.exp(lr * dt)
    ab_re = mag * jnp.cos(li * dt)
    ab_im = mag * jnp.sin(li * dt)
    nr = ab_re - 1.0
    ni = ab_im
    den = lr * lr + li * li
    q_re = (nr * lr + ni * li) / den
    q_im = (ni * lr - nr * li) / den
    are_ref[...] = jnp.broadcast_to(ab_re, are_ref.shape)
    aim_ref[...] = jnp.broadcast_to(ab_im, aim_ref.shape)
    br = br_ref[...]
    bi = bi_ref[...]
    bbr_ref[...] = q_re * br - q_im * bi
    bbi_ref[...] = q_re * bi + q_im * br


def _s5_discretize(lam_re, lam_im, log_dt, b_re, b_im):
    lr = lam_re.reshape(1, N_STATE)
    li = lam_im.reshape(1, N_STATE)
    ldt = jnp.repeat(log_dt, STATE).reshape(1, N_STATE)
    br = jnp.transpose(b_re, (2, 0, 1)).reshape(GROUP, N_STATE)
    bi = jnp.transpose(b_im, (2, 0, 1)).reshape(GROUP, N_STATE)
    return pl.pallas_call(
        _s5_discretize_kernel,
        out_shape=(jax.ShapeDtypeStruct((8, N_STATE), F32),
                   jax.ShapeDtypeStruct((8, N_STATE), F32),
                   jax.ShapeDtypeStruct((GROUP, N_STATE), F32),
                   jax.ShapeDtypeStruct((GROUP, N_STATE), F32)),
        name="s5_discretize",
    )(lr, li, ldt, br, bi)


def _in_proj_kernel(x_ref, g_ref, w_ref, bf_ref, qg_ref, kg_ref, ones_ref, tri_ref,
                    u_ref, q_ref, kt_ref, v_ref, crow_ref, carry_ref):
    @pl.when(pl.program_id(1) == 0)
    def _():
        carry_ref[...] = jnp.zeros_like(carry_ref)

    hn = _rms(x_ref[...], g_ref[...]).astype(BF16)
    proj = jnp.dot(hn, w_ref[...], preferred_element_type=F32)
    u_ref[...] = proj[:, :D_SSM]
    q = proj[:, D_SSM:D_SSM + D_ATTN]
    k = proj[:, D_SSM + D_ATTN:D_SSM + 2 * D_ATTN]
    v = proj[:, D_SSM + 2 * D_ATTN:D_SSM + 3 * D_ATTN]
    f = proj[:, D_SSM + 3 * D_ATTN:]

    def head_norm(z, g):
        ss = jnp.dot((z * z).astype(BF16), ones_ref[...], preferred_element_type=F32)
        return z * lax.rsqrt(ss * (1.0 / HEAD_DIM) + EPS) * g

    q_ref[...] = head_norm(q, qg_ref[...]).astype(BF16)
    kt_ref[...] = head_norm(k, kg_ref[...]).T.astype(BF16)
    v_ref[...] = v.astype(BF16)

    z = f + bf_ref[...]
    logf = jnp.minimum(z, 0.0) - jnp.log1p(jnp.exp(-jnp.abs(z)))
    hi = logf.astype(BF16)
    r1 = logf - hi.astype(F32)
    mid = r1.astype(BF16)
    lo = (r1 - mid.astype(F32)).astype(BF16)
    tri = tri_ref[...]
    cum = (jnp.dot(tri, hi, preferred_element_type=F32)
           + jnp.dot(tri, mid, preferred_element_type=F32)
           + jnp.dot(tri, lo, preferred_element_type=F32)) + carry_ref[0:1, :]
    carry_ref[...] = jnp.broadcast_to(cum[ROWS_IN - 1:ROWS_IN, :], carry_ref.shape)
    crow_ref[...] = cum.T[:N_HEADS, :] * LOG2E


def _in_proj(x, norm_g, w_cat, bf_pad, qg, kg, ones_bd, tri):
    bsz, seq, _ = x.shape
    nt = seq // ROWS_IN
    row = lambda b, t: (b, t, 0)
    return pl.pallas_call(
        _in_proj_kernel,
        grid=(bsz, nt),
        in_specs=[
            pl.BlockSpec((None, ROWS_IN, D_MODEL), row),
            _const_spec((1, D_MODEL)),
            _const_spec((D_MODEL, W_IN_COLS)),
            _const_spec((1, LANES)),
            _const_spec((1, D_ATTN)),
            _const_spec((1, D_ATTN)),
            _const_spec((D_ATTN, D_ATTN)),
            _const_spec((ROWS_IN, ROWS_IN)),
        ],
        out_specs=[
            pl.BlockSpec((None, ROWS_IN, D_SSM), row),
            pl.BlockSpec((None, ROWS_IN, D_ATTN), row),
            pl.BlockSpec((None, D_ATTN, ROWS_IN), lambda b, t: (b, 0, t)),
            pl.BlockSpec((None, ROWS_IN, D_ATTN), row),
            pl.BlockSpec((None, N_HEADS, ROWS_IN), lambda b, t: (b, 0, t)),
        ],
        out_shape=[
            jax.ShapeDtypeStruct((bsz, seq, D_SSM), F32),
            jax.ShapeDtypeStruct((bsz, seq, D_ATTN), BF16),
            jax.ShapeDtypeStruct((bsz, D_ATTN, seq), BF16),
            jax.ShapeDtypeStruct((bsz, seq, D_ATTN), BF16),
            jax.ShapeDtypeStruct((bsz, N_HEADS, seq), F32),
        ],
        scratch_shapes=[pltpu.VMEM((8, LANES), F32)],
        compiler_params=pltpu.CompilerParams(
            dimension_semantics=("arbitrary", "arbitrary"), vmem_limit_bytes=VMEM_LIMIT),
        name="in_proj",
    )(x, norm_g, w_cat, bf_pad, qg, kg, ones_bd, tri)


SCAN_LANES = 512


def _s5_scan_kernel(u_ref, are_ref, aim_ref, wb_ref, wc_ref, dskip_ref, wglu_ref, bglu_ref,
                    y_ref, utb_ref, bu_ref, xs_ref, st_ref, ytb_ref):
    bsz = u_ref.shape[0]
    rows = T_SCAN * bsz

    @pl.when(pl.program_id(0) == 0)
    def _():
        st_ref[...] = jnp.zeros_like(st_ref)

    for t in range(T_SCAN):
        utb_ref[t * bsz:(t + 1) * bsz, :] = u_ref[:, t, :]
    u_tb = utb_ref[...]
    ub = u_tb.astype(BF16)

    n_tiles = N_STATE // 256
    for j in range(2 * n_tiles):
        jj = j % n_tiles
        cs = LANES * (jj // 2)
        bu_ref[:, 256 * j:256 * (j + 1)] = jnp.dot(
            ub[:, cs:cs + LANES], wb_ref[j], preferred_element_type=F32)

    half = bsz // 2
    for c in range(N_STATE // SCAN_LANES):
        re = slice(c * SCAN_LANES, (c + 1) * SCAN_LANES)
        im = slice(N_STATE + c * SCAN_LANES, N_STATE + (c + 1) * SCAN_LANES)
        ar = are_ref[:, re]
        ai = aim_ref[:, re]

        def step(t, carry):
            r0 = pl.multiple_of(t * bsz, bsz)
            bur = bu_ref[pl.ds(r0, bsz), re]
            bui = bu_ref[pl.ds(r0, bsz), im]
            new = []
            for s in range(2):
                xr, xi = carry[2 * s], carry[2 * s + 1]
                rs = slice(s * half, (s + 1) * half)
                nxr = ar * xr - ai * xi + bur[rs]
                nxi = ar * xi + ai * xr + bui[rs]
                new += [nxr, nxi]
            xs_ref[pl.ds(r0, bsz), re] = jnp.concatenate([new[0], new[2]], axis=0).astype(BF16)
            xs_ref[pl.ds(r0, bsz), im] = jnp.concatenate([new[1], new[3]], axis=0).astype(BF16)
            return tuple(new)

        init = (st_ref[0:half, re], st_ref[0:half, im], st_ref[half:bsz, re], st_ref[half:bsz, im])
        fin = lax.fori_loop(0, T_SCAN, step, init, unroll=2)
        st_ref[0:half, re] = fin[0]
        st_ref[0:half, im] = fin[1]
        st_ref[half:bsz, re] = fin[2]
        st_ref[half:bsz, im] = fin[3]

    ys = []
    for m in range(D_SSM // LANES):
        xm = jnp.concatenate(
            [xs_ref[:, 512 * m:512 * (m + 1)], xs_ref[:, N_STATE + 512 * m:N_STATE + 512 * (m + 1)]],
            axis=1)
        ys.append(jnp.dot(xm, wc_ref[m], preferred_element_type=F32))
    y = jnp.concatenate(ys, axis=1) + dskip_ref[...] * u_tb
    y = jax.nn.gelu(y)
    z = jnp.dot(y.astype(BF16), wglu_ref[...], preferred_element_type=F32) + bglu_ref[...]
    ytb_ref[...] = y * jax.nn.sigmoid(z)
    for t in range(T_SCAN):
        y_ref[:, t, :] = ytb_ref[t * bsz:(t + 1) * bsz, :]


def _s5_scan(u, a_re, a_im, wb, wc, d_skip, w_glu, b_glu):
    bsz, seq, _ = u.shape
    rows = T_SCAN * bsz
    blk = lambda t: (0, t, 0)
    return pl.pallas_call(
        _s5_scan_kernel,
        grid=(seq // T_SCAN,),
        in_specs=[
            pl.BlockSpec((bsz, T_SCAN, D_SSM), blk),
            _const_spec((8, N_STATE)),
            _const_spec((8, N_STATE)),
            _const_spec((2 * N_STATE // 256, LANES, 256)),
            _const_spec((D_SSM // LANES, 1024, LANES)),
            _const_spec((1, D_SSM)),
            _const_spec((D_SSM, D_SSM)),
            _const_spec((1, D_SSM)),
        ],
        out_specs=pl.BlockSpec((bsz, T_SCAN, D_SSM), blk),
        out_shape=jax.ShapeDtypeStruct((bsz, seq, D_SSM), F32),
        scratch_shapes=[
            pltpu.VMEM((rows, D_SSM), F32),
            pltpu.VMEM((rows, 2 * N_STATE), F32),
            pltpu.VMEM((rows, 2 * N_STATE), BF16),
            pltpu.VMEM((bsz, 2 * N_STATE), F32),
            pltpu.VMEM((rows, D_SSM), F32),
        ],
        compiler_params=pltpu.CompilerParams(
            dimension_semantics=("arbitrary",), vmem_limit_bytes=VMEM_LIMIT),
        name="s5_scan",
    )(u, a_re, a_im, wb, wc, d_skip, w_glu, b_glu)


NEG_BIG = -1e30


def _fox_kernel(q_ref, kt_ref, v_ref, crow_ref, mask_ref, o_ref):
    seq = q_ref.shape[0]
    hp = pl.program_id(1)
    lane = lax.broadcasted_iota(jnp.int32, (1, LANES), 1)
    for qi in range(seq // TQ):
        q0 = qi * TQ
        klen = q0 + TQ
        q2 = q_ref[q0:q0 + TQ, :]
        outs = []
        for head in range(2):
            hmask = (lane < HEAD_DIM) if head == 0 else (lane >= HEAD_DIM)
            qh = jnp.where(hmask, q2, jnp.zeros_like(q2))
            crow = crow_ref[pl.ds(2 * hp + head, 1), 0:klen]
            rk = crow[:, klen - 1:klen] - crow
            s = jnp.dot(qh, kt_ref[:, 0:klen], preferred_element_type=F32) + rk
            diag = s[:, q0:] + mask_ref[...]
            s = diag if q0 == 0 else jnp.concatenate([s[:, :q0], diag], axis=1)
            m = jnp.max(s, axis=1, keepdims=True)
            p = jnp.exp2(s - m)
            l = jnp.sum(p, axis=1, keepdims=True)
            o = jnp.dot(p.astype(BF16), v_ref[0:klen, :], preferred_element_type=F32)
            outs.append(o / l)
        o_ref[q0:q0 + TQ, :] = jnp.where(lane < HEAD_DIM, outs[0], outs[1]).astype(o_ref.dtype)


def _fox_attention(q, kt, v, crow, mask):
    bsz, seq, _ = q.shape
    n_pairs = D_ATTN // LANES
    return pl.pallas_call(
        _fox_kernel,
        grid=(bsz, n_pairs),
        in_specs=[
            pl.BlockSpec((None, seq, LANES), lambda b, h: (b, 0, h)),
            pl.BlockSpec((None, LANES, seq), lambda b, h: (b, h, 0)),
            pl.BlockSpec((None, seq, LANES), lambda b, h: (b, 0, h)),
            pl.BlockSpec((None, N_HEADS, seq), lambda b, h: (b, 0, 0)),
            _const_spec((TQ, TQ)),
        ],
        out_specs=pl.BlockSpec((None, seq, LANES), lambda b, h: (b, 0, h)),
        out_shape=jax.ShapeDtypeStruct((bsz, seq, D_ATTN), BF16),
        compiler_params=pltpu.CompilerParams(
            dimension_semantics=("arbitrary", "arbitrary"), vmem_limit_bytes=VMEM_LIMIT),
        name="fox_attention",
    )(q, kt, v, crow, mask)


def _out_ffn_kernel(x_ref, ys_ref, ya_ref, gs_ref, ga_ref, wout_ref, gf_ref, wg_ref, wv_ref,
                    cw_ref, cb_ref, wd_ref, o_ref, act_ref, prev_ref):
    @pl.when(pl.program_id(1) == 0)
    def _():
        prev_ref[...] = jnp.zeros_like(prev_ref)

    ms = _rms(ys_ref[...], gs_ref[...])
    ma = _rms(ya_ref[...].astype(F32), ga_ref[...])
    mixed = jnp.concatenate([ms, ma], axis=1).astype(BF16)
    h = x_ref[...] + jnp.dot(mixed, wout_ref[...], preferred_element_type=F32)
    hn = _rms(h, gf_ref[...]).astype(BF16)

    rows = ROWS_FFN

    def conv(u, off):
        ext = jnp.concatenate([prev_ref[:, off:off + FF_CHUNK], u], axis=0)
        prev_ref[:, off:off + FF_CHUNK] = u[rows - 8:, :]
        w = cw_ref[:, off:off + FF_CHUNK]
        return (ext[6:rows + 6] * w[0:1] + ext[7:rows + 7] * w[1:2] + ext[8:] * w[2:3]
                + cb_ref[:, off:off + FF_CHUNK])

    for c in range(D_FF_PAD // FF_CHUNK):
        lo = c * FF_CHUNK
        ug = jnp.dot(hn, wg_ref[:, lo:lo + FF_CHUNK], preferred_element_type=F32)
        uv = jnp.dot(hn, wv_ref[:, lo:lo + FF_CHUNK], preferred_element_type=F32)
        cg = conv(ug, lo)
        cv = conv(uv, D_FF_PAD + lo)
        act_ref[:, lo:lo + FF_CHUNK] = (cg * jax.nn.sigmoid(cg) * cv).astype(BF16)
    o_ref[...] = h + jnp.dot(act_ref[...], wd_ref[...], preferred_element_type=F32)


def _out_ffn(x, y_ssm, y_attn, gs, ga, w_out, gf, wg, wv, cw, cb, wd):
    bsz, seq, _ = x.shape
    row = lambda b, t: (b, t, 0)
    return pl.pallas_call(
        _out_ffn_kernel,
        grid=(bsz, seq // ROWS_FFN),
        in_specs=[
            pl.BlockSpec((None, ROWS_FFN, D_MODEL), row),
            pl.BlockSpec((None, ROWS_FFN, D_SSM), row),
            pl.BlockSpec((None, ROWS_FFN, D_ATTN), row),
            _const_spec((1, D_SSM)),
            _const_spec((1, D_ATTN)),
            _const_spec((D_MODEL, D_MODEL)),
            _const_spec((1, D_MODEL)),
            _const_spec((D_MODEL, D_FF_PAD)),
            _const_spec((D_MODEL, D_FF_PAD)),
            _const_spec((3, 2 * D_FF_PAD)),
            _const_spec((1, 2 * D_FF_PAD)),
            _const_spec((D_FF_PAD, D_MODEL)),
        ],
        out_specs=pl.BlockSpec((None, ROWS_FFN, D_MODEL), row),
        out_shape=jax.ShapeDtypeStruct((bsz, seq, D_MODEL), F32),
        scratch_shapes=[
            pltpu.VMEM((ROWS_FFN, D_FF_PAD), BF16),
            pltpu.VMEM((8, 2 * D_FF_PAD), F32),
        ],
        compiler_params=pltpu.CompilerParams(
            dimension_semantics=("arbitrary", "arbitrary"), vmem_limit_bytes=VMEM_LIMIT),
        name="out_ffn",
    )(x, y_ssm, y_attn, gs, ga, w_out, gf, wg, wv, cw, cb, wd)


def _pad_cols(a, n):
    return jnp.pad(a, ((0, 0), (0, n - a.shape[1])))


def _layer(h, norm_mix, w_in, b_forget, lam_re, lam_im, b_re, b_im, c_re, c_im, d_skip, log_dt,
           w_glu, b_glu, q_norm, k_norm, norm_out_ssm, norm_out_attn, w_out, norm_ffn, w_up,
           conv_w, conv_b, w_down):
    G, P, H = N_GROUPS, STATE, GROUP
    a_re, a_im, bbr, bbi = _s5_discretize(lam_re, lam_im, log_dt, b_re, b_im)
    eye = jnp.eye(G, dtype=F32)

    def b_tiles(bb):
        full = (eye[:, None, :, None] * bb.reshape(H, G, P).transpose(1, 0, 2)[:, :, None, :])
        full = full.reshape(D_SSM // LANES, LANES, N_STATE // 256, 256)
        j = jnp.arange(N_STATE // 256)
        return full[j // 2, :, j, :]

    wb = jnp.concatenate([b_tiles(bbr), b_tiles(bbi)], axis=0).astype(BF16)

    def c_full(cc):
        return (eye[:, None, :, None] * cc.transpose(0, 2, 1)[:, :, None, :]).reshape(N_STATE, D_SSM)

    cr, ci = c_full(c_re), c_full(c_im)
    wc = jnp.stack([
        jnp.concatenate([cr[512 * m:512 * (m + 1), LANES * m:LANES * (m + 1)],
                         -ci[512 * m:512 * (m + 1), LANES * m:LANES * (m + 1)]], axis=0)
        for m in range(D_SSM // LANES)]).astype(BF16)

    w_cat = _pad_cols(w_in, W_IN_COLS).astype(BF16)
    bf_pad = _pad_cols(b_forget.reshape(1, N_HEADS), LANES)
    qg = jnp.tile(q_norm * (HEAD_DIM ** -0.5 * LOG2E), N_HEADS).reshape(1, D_ATTN)
    kg = jnp.tile(k_norm, N_HEADS).reshape(1, D_ATTN)
    head_id = jnp.arange(D_ATTN) // HEAD_DIM
    ones_bd = (head_id[:, None] == head_id[None, :]).astype(BF16)
    r = jnp.arange(ROWS_IN)
    tri = (r[:, None] >= r[None, :]).astype(BF16)

    u, q, kt, v, crow = _in_proj(h, norm_mix.reshape(1, D_MODEL), w_cat, bf_pad, qg, kg, ones_bd, tri)
    y_ssm = _s5_scan(u, a_re, a_im, wb, wc, d_skip.reshape(1, D_SSM), w_glu.astype(BF16),
                     b_glu.reshape(1, D_SSM))
    rq = jnp.arange(TQ)
    causal_bias = jnp.where(rq[:, None] >= rq[None, :], 0.0, NEG_BIG).astype(F32)
    y_attn = _fox_attention(q, kt, v, crow, causal_bias)

    wg = _pad_cols(w_up[:, :D_FF], D_FF_PAD).astype(BF16)
    wv = _pad_cols(w_up[:, D_FF:], D_FF_PAD).astype(BF16)
    cw = jnp.concatenate([_pad_cols(conv_w[:, :D_FF], D_FF_PAD), _pad_cols(conv_w[:, D_FF:], D_FF_PAD)], axis=1)
    cbp = jnp.concatenate([_pad_cols(conv_b[None, :D_FF], D_FF_PAD), _pad_cols(conv_b[None, D_FF:], D_FF_PAD)], axis=1)
    wd = jnp.pad(w_down, ((0, D_FF_PAD - D_FF), (0, 0))).astype(BF16)
    return _out_ffn(h, y_ssm, y_attn, norm_out_ssm.reshape(1, D_SSM), norm_out_attn.reshape(1, D_ATTN),
                    w_out.astype(BF16), norm_ffn.reshape(1, D_MODEL), wg, wv, cw, cbp, wd)


def kernel(x, norm_mix, w_in, b_forget, lam_re, lam_im, b_re, b_im, c_re, c_im, d_skip, log_dt,
           w_glu, b_glu, q_norm, k_norm, norm_out_ssm, norm_out_attn, w_out, norm_ffn, w_up, conv_w,
           conv_b, w_down):
    h = x
    for l in range(norm_mix.shape[0]):
        h = _layer(h, norm_mix[l], w_in[l], b_forget[l], lam_re[l], lam_im[l], b_re[l], b_im[l],
                   c_re[l], c_im[l], d_skip[l], log_dt[l], w_glu[l], b_glu[l], q_norm[l], k_norm[l],
                   norm_out_ssm[l], norm_out_attn[l], w_out[l], norm_ffn[l], w_up[l], conv_w[l],
                   conv_b[l], w_down[l])
    return h
```

```python
import functools

import jax
import jax.numpy as jnp
from jax import lax
from jax.experimental import pallas as pl
from jax.experimental.pallas import tpu as pltpu

F32 = jnp.float32
BF16 = jnp.bfloat16

D_MODEL = 1024
D_SSM = 512
N_GROUPS = 32
GROUP = 16
STATE = 64
N_STATE = N_GROUPS * STATE
D_ATTN = 512
N_HEADS = 8
HEAD_DIM = 64
D_FF = 2752
EPS = 1e-6
LOG2E = 1.4426950408889634

LANES = 128
D_FF_PAD = 2816
W_IN_COLS = D_SSM + 3 * D_ATTN + LANES

ROWS_IN = 512
T_SCAN = 32
TQ = 256
ROWS_FFN = 512
FF_CHUNK = 256

VMEM_LIMIT = 56 * 1024 * 1024


def _const_spec(shape):
    nd = len(shape)
    return pl.BlockSpec(shape, lambda *_: (0,) * nd, pipeline_mode=pl.Buffered(1))


def _rms(x, g):
    return x * lax.rsqrt(jnp.mean(x * x, axis=-1, keepdims=True) + EPS) * g


def _s5_discretize_kernel(lr_ref, li_ref, ldt_ref, br_ref, bi_ref,
                          are_ref, aim_ref, bbr_ref, bbi_ref):
    lr = lr_ref[...]
    li = li_ref[...]
    dt = jnp.exp(ldt_ref[...])
    mag = jnp.exp(lr * dt)
    ab_re = mag * jnp.cos(li * dt)
    ab_im = mag * jnp.sin(li * dt)
    nr = ab_re - 1.0
    ni = ab_im
    den = lr * lr + li * li
    q_re = (nr * lr + ni * li) / den
    q_im = (ni * lr - nr * li) / den
    are_ref[...] = jnp.broadcast_to(ab_re, are_ref.shape)
    aim_ref[...] = jnp.broadcast_to(ab_im, aim_ref.shape)
    br = br_ref[...]
    bi = bi_ref[...]
    bbr_ref[...] = q_re * br - q_im * bi
    bbi_ref[...] = q_re * bi + q_im * br


def _s5_discretize(lam_re, lam_im, log_dt, b_re, b_im):
    lr = lam_re.reshape(1, N_STATE)
    li = lam_im.reshape(1, N_STATE)
    ldt = jnp.repeat(log_dt, STATE).reshape(1, N_STATE)
    br = jnp.transpose(b_re, (2, 0, 1)).reshape(GROUP, N_STATE)
    bi = jnp.transpose(b_im, (2, 0, 1)).reshape(GROUP, N_STATE)
    return pl.pallas_call(
        _s5_discretize_kernel,
        out_shape=(jax.ShapeDtypeStruct((8, N_STATE), F32),
                   jax.ShapeDtypeStruct((8, N_STATE), F32),
                   jax.ShapeDtypeStruct((GROUP, N_STATE), F32),
                   jax.ShapeDtypeStruct((GROUP, N_STATE), F32)),
        name="s5_discretize",
    )(lr, li, ldt, br, bi)


def _in_proj_kernel(x_ref, g_ref, w_ref, bf_ref, qg_ref, kg_ref, ones_ref, tri_ref,
                    u_ref, q_ref, kt_ref, v_ref, crow_ref, carry_ref):
    @pl.when(pl.program_id(1) == 0)
    def _():
        carry_ref[...] = jnp.zeros_like(carry_ref)

    hn = _rms(x_ref[...], g_ref[...]).astype(BF16)
    proj = jnp.dot(hn, w_ref[...], preferred_element_type=F32)
    u_ref[...] = proj[:, :D_SSM]
    q = proj[:, D_SSM:D_SSM + D_ATTN]
    k = proj[:, D_SSM + D_ATTN:D_SSM + 2 * D_ATTN]
    v = proj[:, D_SSM + 2 * D_ATTN:D_SSM + 3 * D_ATTN]
    f = proj[:, D_SSM + 3 * D_ATTN:]

    def head_norm(z, g):
        ss = jnp.dot((z * z).astype(BF16), ones_ref[...], preferred_element_type=F32)
        return z * lax.rsqrt(ss * (1.0 / HEAD_DIM) + EPS) * g

    q_ref[...] = head_norm(q, qg_ref[...]).astype(BF16)
    kt_ref[...] = head_norm(k, kg_ref[...]).T.astype(BF16)
    v_ref[...] = v.astype(BF16)

    z = f + bf_ref[...]
    logf = jnp.minimum(z, 0.0) - jnp.log1p(jnp.exp(-jnp.abs(z)))
    hi = logf.astype(BF16)
    r1 = logf - hi.astype(F32)
    mid = r1.astype(BF16)
    lo = (r1 - mid.astype(F32)).astype(BF16)
    tri = tri_ref[...]
    cum = (jnp.dot(tri, hi, preferred_element_type=F32)
           + jnp.dot(tri, mid, preferred_element_type=F32)
           + jnp.dot(tri, lo, preferred_element_type=F32)) + carry_ref[0:1, :]
    carry_ref[...] = jnp.broadcast_to(cum[ROWS_IN - 1:ROWS_IN, :], carry_ref.shape)
    crow_ref[...] = cum.T[:N_HEADS, :] * LOG2E


def _in_proj(x, norm_g, w_cat, bf_pad, qg, kg, ones_bd, tri):
    bsz, seq, _ = x.shape
    nt = seq // ROWS_IN
    row = lambda b, t: (b, t, 0)
    return pl.pallas_call(
        _in_proj_kernel,
        grid=(bsz, nt),
        in_specs=[
            pl.BlockSpec((None, ROWS_IN, D_MODEL), row),
            _const_spec((1, D_MODEL)),
            _const_spec((D_MODEL, W_IN_COLS)),
            _const_spec((1, LANES)),
            _const_spec((1, D_ATTN)),
            _const_spec((1, D_ATTN)),
            _const_spec((D_ATTN, D_ATTN)),
            _const_spec((ROWS_IN, ROWS_IN)),
        ],
        out_specs=[
            pl.BlockSpec((None, ROWS_IN, D_SSM), row),
            pl.BlockSpec((None, ROWS_IN, D_ATTN), row),
            pl.BlockSpec((None, D_ATTN, ROWS_IN), lambda b, t: (b, 0, t)),
            pl.BlockSpec((None, ROWS_IN, D_ATTN), row),
            pl.BlockSpec((None, N_HEADS, ROWS_IN), lambda b, t: (b, 0, t)),
        ],
        out_shape=[
            jax.ShapeDtypeStruct((bsz, seq, D_SSM), F32),
            jax.ShapeDtypeStruct((bsz, seq, D_ATTN), BF16),
            jax.ShapeDtypeStruct((bsz, D_ATTN, seq), BF16),
            jax.ShapeDtypeStruct((bsz, seq, D_ATTN), BF16),
            jax.ShapeDtypeStruct((bsz, N_HEADS, seq), F32),
        ],
        scratch_shapes=[pltpu.VMEM((8, LANES), F32)],
        compiler_params=pltpu.CompilerParams(
            dimension_semantics=("arbitrary", "arbitrary"), vmem_limit_bytes=VMEM_LIMIT),
        name="in_proj",
    )(x, norm_g, w_cat, bf_pad, qg, kg, ones_bd, tri)


SCAN_LANES = 512


def _s5_scan_kernel(u_ref, are_ref, aim_ref, wb_ref, wc_ref, dskip_ref, wglu_ref, bglu_ref,
                    y_ref, utb_ref, bu_ref, xs_ref, st_ref, ytb_ref):
    bsz = u_ref.shape[0]
    rows = T_SCAN * bsz

    @pl.when(pl.program_id(0) == 0)
    def _():
        st_ref[...] = jnp.zeros_like(st_ref)

    for t in range(T_SCAN):
        utb_ref[t * bsz:(t + 1) * bsz, :] = u_ref[:, t, :]
    u_tb = utb_ref[...]
    ub = u_tb.astype(BF16)

    n_tiles = N_STATE // 256
    for j in range(2 * n_tiles):
        jj = j % n_tiles
        cs = LANES * (jj // 2)
        bu_ref[:, 256 * j:256 * (j + 1)] = jnp.dot(
            ub[:, cs:cs + LANES], wb_ref[j], preferred_element_type=F32)

    half = bsz // 2
    for c in range(N_STATE // SCAN_LANES):
        re = slice(c * SCAN_LANES, (c + 1) * SCAN_LANES)
        im = slice(N_STATE + c * SCAN_LANES, N_STATE + (c + 1) * SCAN_LANES)
        ar = are_ref[:, re]
        ai = aim_ref[:, re]

        def step(t, carry):
            r0 = pl.multiple_of(t * bsz, bsz)
            bur = bu_ref[pl.ds(r0, bsz), re]
            bui = bu_ref[pl.ds(r0, bsz), im]
            new = []
            for s in range(2):
                xr, xi = carry[2 * s], carry[2 * s + 1]
                rs = slice(s * half, (s + 1) * half)
                nxr = ar * xr - ai * xi + bur[rs]
                nxi = ar * xi + ai * xr + bui[rs]
                new += [nxr, nxi]
            xs_ref[pl.ds(r0, bsz), re] = jnp.concatenate([new[0], new[2]], axis=0).astype(BF16)
            xs_ref[pl.ds(r0, bsz), im] = jnp.concatenate([new[1], new[3]], axis=0).astype(BF16)
            return tuple(new)

        init = (st_ref[0:half, re], st_ref[0:half, im], st_ref[half:bsz, re], st_ref[half:bsz, im])
        fin = lax.fori_loop(0, T_SCAN, step, init, unroll=2)
        st_ref[0:half, re] = fin[0]
        st_ref[0:half, im] = fin[1]
        st_ref[half:bsz, re] = fin[2]
        st_ref[half:bsz, im] = fin[3]

    ys = []
    for m in range(D_SSM // LANES):
        xm = jnp.concatenate(
            [xs_ref[:, 512 * m:512 * (m + 1)], xs_ref[:, N_STATE + 512 * m:N_STATE + 512 * (m + 1)]],
            axis=1)
        ys.append(jnp.dot(xm, wc_ref[m], preferred_element_type=F32))
    y = jnp.concatenate(ys, axis=1) + dskip_ref[...] * u_tb
    y = jax.nn.gelu(y)
    z = jnp.dot(y.astype(BF16), wglu_ref[...], preferred_element_type=F32) + bglu_ref[...]
    ytb_ref[...] = y * jax.nn.sigmoid(z)
    for t in range(T_SCAN):
        y_ref[:, t, :] = ytb_ref[t * bsz:(t + 1) * bsz, :]


def _s5_scan(u, a_re, a_im, wb, wc, d_skip, w_glu, b_glu):
    bsz, seq, _ = u.shape
    rows = T_SCAN * bsz
    blk = lambda t: (0, t, 0)
    return pl.pallas_call(
        _s5_scan_kernel,
        grid=(seq // T_SCAN,),
        in_specs=[
            pl.BlockSpec((bsz, T_SCAN, D_SSM), blk),
            _const_spec((8, N_STATE)),
            _const_spec((8, N_STATE)),
            _const_spec((2 * N_STATE // 256, LANES, 256)),
            _const_spec((D_SSM // LANES, 1024, LANES)),
            _const_spec((1, D_SSM)),
            _const_spec((D_SSM, D_SSM)),
            _const_spec((1, D_SSM)),
        ],
        out_specs=pl.BlockSpec((bsz, T_SCAN, D_SSM), blk),
        out_shape=jax.ShapeDtypeStruct((bsz, seq, D_SSM), F32),
        scratch_shapes=[
            pltpu.VMEM((rows, D_SSM), F32),
            pltpu.VMEM((rows, 2 * N_STATE), F32),
            pltpu.VMEM((rows, 2 * N_STATE), BF16),
            pltpu.VMEM((bsz, 2 * N_STATE), F32),
            pltpu.VMEM((rows, D_SSM), F32),
        ],
        compiler_params=pltpu.CompilerParams(
            dimension_semantics=("arbitrary",), vmem_limit_bytes=VMEM_LIMIT),
        name="s5_scan",
    )(u, a_re, a_im, wb, wc, d_skip, w_glu, b_glu)


NEG_BIG = -1e30


def _lane_tile_reduce(x, op):
    acc = x[:, :LANES]
    for i in range(1, x.shape[1] // LANES):
        acc = op(acc, x[:, i * LANES:(i + 1) * LANES])
    return acc


def _fox_kernel(q_ref, kt_ref, v_ref, crow_ref, mask_ref, o_ref, s_ref, p_ref, va_ref):
    seq = q_ref.shape[0]
    n_pairs = q_ref.shape[1] // LANES
    lane = lax.broadcasted_iota(jnp.int32, (1, LANES), 1)
    head_lanes = (lane < HEAD_DIM, lane >= HEAD_DIM)
    for hp in range(n_pairs):
        v2 = v_ref[:, hp * LANES:(hp + 1) * LANES]
        for head in range(2):
            va_ref[2 * hp + head] = jnp.where(head_lanes[head], v2, jnp.ones_like(v2))

    def scores(qi, hp, head, slot):
        q0 = qi * TQ
        klen = q0 + TQ
        q2 = q_ref[q0:q0 + TQ, hp * LANES:(hp + 1) * LANES]
        qh = jnp.where(head_lanes[head], q2, jnp.zeros_like(q2))
        crow = crow_ref[2 * hp + head:2 * hp + head + 1, 0:klen]
        rk = crow[:, klen - 1:klen] - crow
        m_run = None
        for c0 in range(0, klen, TQ):
            s = jnp.dot(qh, kt_ref[hp * LANES:(hp + 1) * LANES, c0:c0 + TQ],
                        preferred_element_type=F32) + rk[:, c0:c0 + TQ]
            if c0 == q0:
                s = s + mask_ref[...]
            s_ref[slot, :, c0:c0 + TQ] = s
            cm = _lane_tile_reduce(s, jnp.maximum)
            m_run = cm if m_run is None else jnp.maximum(m_run, cm)
        return jnp.max(m_run, axis=1, keepdims=True)

    def weighted_values(qi, hp, head, slot, m):
        klen = (qi + 1) * TQ
        for c0 in range(0, klen, TQ):
            p_ref[slot, :, c0:c0 + TQ] = jnp.exp2(s_ref[slot, :, c0:c0 + TQ] - m).astype(BF16)
        o = jnp.dot(p_ref[slot, :, 0:klen], va_ref[2 * hp + head, 0:klen, :], preferred_element_type=F32)
        l_col = HEAD_DIM * (1 - head)
        return o / o[:, l_col:l_col + 1]

    work = [(qi, hp, head) for qi in reversed(range(seq // TQ)) for hp in range(n_pairs)
            for head in range(2)]
    slots = s_ref.shape[0]
    m_next = scores(*work[0], 0)
    outs = []
    for i, (qi, hp, head) in enumerate(work):
        m_cur = m_next
        if i + 1 < len(work):
            m_next = scores(*work[i + 1], (i + 1) % slots)
        outs.append(weighted_values(qi, hp, head, i % slots, m_cur))
        if head == 1:
            o_ref[qi * TQ:(qi + 1) * TQ, hp * LANES:(hp + 1) * LANES] = jnp.where(
                head_lanes[0], outs[0], outs[1]).astype(o_ref.dtype)
            outs = []


def _fox_attention(q, kt, v, crow, mask):
    bsz, seq, _ = q.shape
    return pl.pallas_call(
        _fox_kernel,
        grid=(bsz,),
        in_specs=[
            pl.BlockSpec((None, seq, D_ATTN), lambda b: (b, 0, 0)),
            pl.BlockSpec((None, D_ATTN, seq), lambda b: (b, 0, 0)),
            pl.BlockSpec((None, seq, D_ATTN), lambda b: (b, 0, 0)),
            pl.BlockSpec((None, N_HEADS, seq), lambda b: (b, 0, 0)),
            _const_spec((TQ, TQ)),
        ],
        out_specs=pl.BlockSpec((None, seq, D_ATTN), lambda b: (b, 0, 0)),
        out_shape=jax.ShapeDtypeStruct((bsz, seq, D_ATTN), BF16),
        scratch_shapes=[
            pltpu.VMEM((2, TQ, seq), F32),
            pltpu.VMEM((2, TQ, seq), BF16),
            pltpu.VMEM((N_HEADS, seq, LANES), BF16),
        ],
        compiler_params=pltpu.CompilerParams(
            dimension_semantics=("arbitrary",), vmem_limit_bytes=VMEM_LIMIT),
        name="fox_attention",
    )(q, kt, v, crow, mask)


def _out_ffn_kernel(x_ref, ys_ref, ya_ref, gs_ref, ga_ref, wout_ref, gf_ref, wg_ref, wv_ref,
                    cw_ref, cb_ref, wd_ref, o_ref, act_ref, prev_ref):
    @pl.when(pl.program_id(1) == 0)
    def _():
        prev_ref[...] = jnp.zeros_like(prev_ref)

    ms = _rms(ys_ref[...], gs_ref[...])
    ma = _rms(ya_ref[...].astype(F32), ga_ref[...])
    mixed = jnp.concatenate([ms, ma], axis=1).astype(BF16)
    h = x_ref[...] + jnp.dot(mixed, wout_ref[...], preferred_element_type=F32)
    hn = _rms(h, gf_ref[...]).astype(BF16)

    rows = ROWS_FFN

    def conv(u, off):
        ext = jnp.concatenate([prev_ref[:, off:off + FF_CHUNK], u], axis=0)
        prev_ref[:, off:off + FF_CHUNK] = u[rows - 8:, :]
        w = cw_ref[:, off:off + FF_CHUNK]
        return (ext[6:rows + 6] * w[0:1] + ext[7:rows + 7] * w[1:2] + ext[8:] * w[2:3]
                + cb_ref[:, off:off + FF_CHUNK])

    for c in range(D_FF_PAD // FF_CHUNK):
        lo = c * FF_CHUNK
        ug = jnp.dot(hn, wg_ref[:, lo:lo + FF_CHUNK], preferred_element_type=F32)
        uv = jnp.dot(hn, wv_ref[:, lo:lo + FF_CHUNK], preferred_element_type=F32)
        cg = conv(ug, lo)
        cv = conv(uv, D_FF_PAD + lo)
        act_ref[:, lo:lo + FF_CHUNK] = (cg * jax.nn.sigmoid(cg) * cv).astype(BF16)
    o_ref[...] = h + jnp.dot(act_ref[...], wd_ref[...], preferred_element_type=F32)


def _out_ffn(x, y_ssm, y_attn, gs, ga, w_out, gf, wg, wv, cw, cb, wd):
    bsz, seq, _ = x.shape
    row = lambda b, t: (b, t, 0)
    return pl.pallas_call(
        _out_ffn_kernel,
        grid=(bsz, seq // ROWS_FFN),
        in_specs=[
            pl.BlockSpec((None, ROWS_FFN, D_MODEL), row),
            pl.BlockSpec((None, ROWS_FFN, D_SSM), row),
            pl.BlockSpec((None, ROWS_FFN, D_ATTN), row),
            _const_spec((1, D_SSM)),
            _const_spec((1, D_ATTN)),
            _const_spec((D_MODEL, D_MODEL)),
            _const_spec((1, D_MODEL)),
            _const_spec((D_MODEL, D_FF_PAD)),
            _const_spec((D_MODEL, D_FF_PAD)),
            _const_spec((3, 2 * D_FF_PAD)),
            _const_spec((1, 2 * D_FF_PAD)),
            _const_spec((D_FF_PAD, D_MODEL)),
        ],
        out_specs=pl.BlockSpec((None, ROWS_FFN, D_MODEL), row),
        out_shape=jax.ShapeDtypeStruct((bsz, seq, D_MODEL), F32),
        scratch_shapes=[
            pltpu.VMEM((ROWS_FFN, D_FF_PAD), BF16),
            pltpu.VMEM((8, 2 * D_FF_PAD), F32),
        ],
        compiler_params=pltpu.CompilerParams(
            dimension_semantics=("arbitrary", "arbitrary"), vmem_limit_bytes=VMEM_LIMIT),
        name="out_ffn",
    )(x, y_ssm, y_attn, gs, ga, w_out, gf, wg, wv, cw, cb, wd)


def _pad_cols(a, n):
    return jnp.pad(a, ((0, 0), (0, n - a.shape[1])))


def _layer(h, norm_mix, w_in, b_forget, lam_re, lam_im, b_re, b_im, c_re, c_im, d_skip, log_dt,
           w_glu, b_glu, q_norm, k_norm, norm_out_ssm, norm_out_attn, w_out, norm_ffn, w_up,
           conv_w, conv_b, w_down):
    G, P, H = N_GROUPS, STATE, GROUP
    a_re, a_im, bbr, bbi = _s5_discretize(lam_re, lam_im, log_dt, b_re, b_im)
    eye = jnp.eye(G, dtype=F32)

    def b_tiles(bb):
        full = (eye[:, None, :, None] * bb.reshape(H, G, P).transpose(1, 0, 2)[:, :, None, :])
        full = full.reshape(D_SSM // LANES, LANES, N_STATE // 256, 256)
        j = jnp.arange(N_STATE // 256)
        return full[j // 2, :, j, :]

    wb = jnp.concatenate([b_tiles(bbr), b_tiles(bbi)], axis=0).astype(BF16)

    def c_full(cc):
        return (eye[:, None, :, None] * cc.transpose(0, 2, 1)[:, :, None, :]).reshape(N_STATE, D_SSM)

    cr, ci = c_full(c_re), c_full(c_im)
    wc = jnp.stack([
        jnp.concatenate([cr[512 * m:512 * (m + 1), LANES * m:LANES * (m + 1)],
                         -ci[512 * m:512 * (m + 1), LANES * m:LANES * (m + 1)]], axis=0)
        for m in range(D_SSM // LANES)]).astype(BF16)

    w_cat = _pad_cols(w_in, W_IN_COLS).astype(BF16)
    bf_pad = _pad_cols(b_forget.reshape(1, N_HEADS), LANES)
    qg = jnp.tile(q_norm * (HEAD_DIM ** -0.5 * LOG2E), N_HEADS).reshape(1, D_ATTN)
    kg = jnp.tile(k_norm, N_HEADS).reshape(1, D_ATTN)
    head_id = jnp.arange(D_ATTN) // HEAD_DIM
    ones_bd = (head_id[:, None] == head_id[None, :]).astype(BF16)
    r = jnp.arange(ROWS_IN)
    tri = (r[:, None] >= r[None, :]).astype(BF16)

    u, q, kt, v, crow = _in_proj(h, norm_mix.reshape(1, D_MODEL), w_cat, bf_pad, qg, kg, ones_bd, tri)
    y_ssm = _s5_scan(u, a_re, a_im, wb, wc, d_skip.reshape(1, D_SSM), w_glu.astype(BF16),
                     b_glu.reshape(1, D_SSM))
    rq = jnp.arange(TQ)
    causal_bias = jnp.where(rq[:, None] >= rq[None, :], 0.0, NEG_BIG).astype(F32)
    y_attn = _fox_attention(q, kt, v, crow, causal_bias)

    wg = _pad_cols(w_up[:, :D_FF], D_FF_PAD).astype(BF16)
    wv = _pad_cols(w_up[:, D_FF:], D_FF_PAD).astype(BF16)
    cw = jnp.concatenate([_pad_cols(conv_w[:, :D_FF], D_FF_PAD), _pad_cols(conv_w[:, D_FF:], D_FF_PAD)], axis=1)
    cbp = jnp.concatenate([_pad_cols(conv_b[None, :D_FF], D_FF_PAD), _pad_cols(conv_b[None, D_FF:], D_FF_PAD)], axis=1)
    wd = jnp.pad(w_down, ((0, D_FF_PAD - D_FF), (0, 0))).astype(BF16)
    return _out_ffn(h, y_ssm, y_attn, norm_out_ssm.reshape(1, D_SSM), norm_out_attn.reshape(1, D_ATTN),
                    w_out.astype(BF16), norm_ffn.reshape(1, D_MODEL), wg, wv, cw, cbp, wd)


def kernel(x, norm_mix, w_in, b_forget, lam_re, lam_im, b_re, b_im, c_re, c_im, d_skip, log_dt,
           w_glu, b_glu, q_norm, k_norm, norm_out_ssm, norm_out_attn, w_out, norm_ffn, w_up, conv_w,
           conv_b, w_down):
    h = x
    for l in range(norm_mix.shape[0]):
        h = _layer(h, norm_mix[l], w_in[l], b_forget[l], lam_re[l], lam_im[l], b_re[l], b_im[l],
                   c_re[l], c_im[l], d_skip[l], log_dt[l], w_glu[l], b_glu[l], q_norm[l], k_norm[l],
                   norm_out_ssm[l], norm_out_attn[l], w_out[l], norm_ffn[l], w_up[l], conv_w[l],
                   conv_b[l], w_down[l])
    return h
```

```python
import functools

import jax
import jax.numpy as jnp
from jax import lax
from jax.experimental import pallas as pl
from jax.experimental.pallas import tpu as pltpu

F32 = jnp.float32
BF16 = jnp.bfloat16

D_MODEL = 1024
D_SSM = 512
N_GROUPS = 32
GROUP = 16
STATE = 64
N_STATE = N_GROUPS * STATE
D_ATTN = 512
N_HEADS = 8
HEAD_DIM = 64
D_FF = 2752
EPS = 1e-6
LOG2E = 1.4426950408889634

LANES = 128
D_FF_PAD = 2816
W_IN_COLS = D_SSM + 3 * D_ATTN + LANES

ROWS_IN = 512
T_SCAN = 32
TQ = 256
ROWS_FFN = 512
FF_CHUNK = 256

VMEM_LIMIT = 56 * 1024 * 1024


def _const_spec(shape):
    nd = len(shape)
    return pl.BlockSpec(shape, lambda *_: (0,) * nd, pipeline_mode=pl.Buffered(1))


def _rms(x, g):
    return x * lax.rsqrt(jnp.mean(x * x, axis=-1, keepdims=True) + EPS) * g


def _s5_discretize_kernel(lr_ref, li_ref, ldt_ref, br_ref, bi_ref, cr_ref, ci_ref, bmask_ref, cmask_ref,
                          expand_ref, are_ref, aim_ref, wb_ref, wc_ref):
    lr = lr_ref[...]
    li = li_ref[...]
    dt = jnp.exp(ldt_ref[...])
    mag = jnp.exp(lr * dt)
    ab_re = mag * jnp.cos(li * dt)
    ab_im = mag * jnp.sin(li * dt)
    nr = ab_re - 1.0
    ni = ab_im
    den = lr * lr + li * li
    q_re = (nr * lr + ni * li) / den
    q_im = (ni * lr - nr * li) / den
    are_ref[...] = jnp.broadcast_to(ab_re, are_ref.shape)
    aim_ref[...] = jnp.broadcast_to(ab_im, aim_ref.shape)
    br = br_ref[...]
    bi = bi_ref[...]
    bb = (q_re * br - q_im * bi, q_re * bi + q_im * br)
    n_tiles = N_STATE // 256
    reps = LANES // GROUP
    for part in range(2):
        for j in range(n_tiles):
            blk = bb[part][:, 256 * j:256 * (j + 1)]
            wb_ref[part * n_tiles + j] = (jnp.concatenate([blk] * reps, axis=0) * bmask_ref[j % 2]).astype(BF16)
    for m in range(D_SSM // LANES):
        rows = slice(512 * m, 512 * (m + 1))
        cr = jnp.dot(cr_ref[rows, :].astype(BF16), expand_ref[...], preferred_element_type=F32)
        ci = jnp.dot(ci_ref[rows, :].astype(BF16), expand_ref[...], preferred_element_type=F32)
        wc_ref[m, 0:512, :] = (cr * cmask_ref[...]).astype(BF16)
        wc_ref[m, 512:1024, :] = (-ci * cmask_ref[...]).astype(BF16)


def _s5_discretize(lam_re, lam_im, log_dt, b_re, b_im, c_re, c_im):
    lr = lam_re.reshape(1, N_STATE)
    li = lam_im.reshape(1, N_STATE)
    ldt = jnp.repeat(log_dt, STATE).reshape(1, N_STATE)
    br = jnp.transpose(b_re, (2, 0, 1)).reshape(GROUP, N_STATE)
    bi = jnp.transpose(b_im, (2, 0, 1)).reshape(GROUP, N_STATE)
    cr = jnp.transpose(c_re, (0, 2, 1)).reshape(N_STATE, GROUP)
    ci = jnp.transpose(c_im, (0, 2, 1)).reshape(N_STATE, GROUP)
    n_tiles = N_STATE // 256
    r = jnp.arange(LANES)[:, None] // GROUP
    c = jnp.arange(256)[None, :] // STATE
    bmask = jnp.stack([(r == c), (r == c + 256 // STATE)]).astype(F32)
    cmask = ((jnp.arange(512)[:, None] // STATE) == (jnp.arange(LANES)[None, :] // GROUP)).astype(F32)
    expand = (jnp.arange(GROUP)[:, None] == (jnp.arange(LANES)[None, :] % GROUP)).astype(BF16)
    return pl.pallas_call(
        _s5_discretize_kernel,
        out_shape=(jax.ShapeDtypeStruct((8, N_STATE), F32),
                   jax.ShapeDtypeStruct((8, N_STATE), F32),
                   jax.ShapeDtypeStruct((2 * n_tiles, LANES, 256), BF16),
                   jax.ShapeDtypeStruct((D_SSM // LANES, 1024, LANES), BF16)),
        name="s5_discretize",
    )(lr, li, ldt, br, bi, cr, ci, bmask, cmask, expand)


def _in_proj_kernel(x_ref, g_ref, w_ref, bf_ref, qg_ref, kg_ref, ones_ref, tri_ref,
                    u_ref, q_ref, kt_ref, v_ref, crow_ref, carry_ref):
    @pl.when(pl.program_id(1) == 0)
    def _():
        carry_ref[...] = jnp.zeros_like(carry_ref)

    hn = _rms(x_ref[...], g_ref[...]).astype(BF16)
    proj = jnp.dot(hn, w_ref[...], preferred_element_type=F32)
    u_ref[...] = proj[:, :D_SSM]
    q = proj[:, D_SSM:D_SSM + D_ATTN]
    k = proj[:, D_SSM + D_ATTN:D_SSM + 2 * D_ATTN]
    v = proj[:, D_SSM + 2 * D_ATTN:D_SSM + 3 * D_ATTN]
    f = proj[:, D_SSM + 3 * D_ATTN:]

    def head_norm(z, g):
        ss = jnp.dot((z * z).astype(BF16), ones_ref[...], preferred_element_type=F32)
        return z * lax.rsqrt(ss * (1.0 / HEAD_DIM) + EPS) * g

    q_ref[...] = head_norm(q, qg_ref[...]).astype(BF16)
    kt_ref[...] = head_norm(k, kg_ref[...]).T.astype(BF16)
    v_ref[...] = v.astype(BF16)

    z = f + bf_ref[...]
    logf = jnp.minimum(z, 0.0) - jnp.log1p(jnp.exp(-jnp.abs(z)))
    hi = logf.astype(BF16)
    r1 = logf - hi.astype(F32)
    mid = r1.astype(BF16)
    lo = (r1 - mid.astype(F32)).astype(BF16)
    tri = tri_ref[...]
    cum = (jnp.dot(tri, hi, preferred_element_type=F32)
           + jnp.dot(tri, mid, preferred_element_type=F32)
           + jnp.dot(tri, lo, preferred_element_type=F32)) + carry_ref[0:1, :]
    carry_ref[...] = jnp.broadcast_to(cum[ROWS_IN - 1:ROWS_IN, :], carry_ref.shape)
    crow_ref[...] = cum.T[:N_HEADS, :] * LOG2E


def _in_proj(x, norm_g, w_cat, bf_pad, qg, kg, ones_bd, tri):
    bsz, seq, _ = x.shape
    nt = seq // ROWS_IN
    row = lambda b, t: (b, t, 0)
    return pl.pallas_call(
        _in_proj_kernel,
        grid=(bsz, nt),
        in_specs=[
            pl.BlockSpec((None, ROWS_IN, D_MODEL), row),
            _const_spec((1, D_MODEL)),
            _const_spec((D_MODEL, W_IN_COLS)),
            _const_spec((1, LANES)),
            _const_spec((1, D_ATTN)),
            _const_spec((1, D_ATTN)),
            _const_spec((D_ATTN, D_ATTN)),
            _const_spec((ROWS_IN, ROWS_IN)),
        ],
        out_specs=[
            pl.BlockSpec((None, ROWS_IN, D_SSM), row),
            pl.BlockSpec((None, ROWS_IN, D_ATTN), row),
            pl.BlockSpec((None, D_ATTN, ROWS_IN), lambda b, t: (b, 0, t)),
            pl.BlockSpec((None, ROWS_IN, D_ATTN), row),
            pl.BlockSpec((None, N_HEADS, ROWS_IN), lambda b, t: (b, 0, t)),
        ],
        out_shape=[
            jax.ShapeDtypeStruct((bsz, seq, D_SSM), F32),
            jax.ShapeDtypeStruct((bsz, seq, D_ATTN), BF16),
            jax.ShapeDtypeStruct((bsz, D_ATTN, seq), BF16),
            jax.ShapeDtypeStruct((bsz, seq, D_ATTN), BF16),
            jax.ShapeDtypeStruct((bsz, N_HEADS, seq), F32),
        ],
        scratch_shapes=[pltpu.VMEM((8, LANES), F32)],
        compiler_params=pltpu.CompilerParams(
            dimension_semantics=("arbitrary", "arbitrary"), vmem_limit_bytes=VMEM_LIMIT),
        name="in_proj",
    )(x, norm_g, w_cat, bf_pad, qg, kg, ones_bd, tri)


SCAN_LANES = 512


def _s5_scan_kernel(u_ref, are_ref, aim_ref, wb_ref, wc_ref, dskip_ref, wglu_ref, bglu_ref,
                    y_ref, utb_ref, bu_ref, xs_ref, st_ref, ytb_ref):
    bsz = u_ref.shape[0]
    rows = T_SCAN * bsz

    @pl.when(pl.program_id(0) == 0)
    def _():
        st_ref[...] = jnp.zeros_like(st_ref)

    for t in range(T_SCAN):
        utb_ref[t * bsz:(t + 1) * bsz, :] = u_ref[:, t, :]
    u_tb = utb_ref[...]
    ub = u_tb.astype(BF16)

    n_tiles = N_STATE // 256
    half = bsz // 2

    def b_proj(c):
        for j in (2 * c, 2 * c + 1, n_tiles + 2 * c, n_tiles + 2 * c + 1):
            bu_ref[:, 256 * j:256 * (j + 1)] = jnp.dot(
                ub[:, LANES * c:LANES * (c + 1)], wb_ref[j], preferred_element_type=F32)

    def recurrence(c):
        re = slice(c * SCAN_LANES, (c + 1) * SCAN_LANES)
        im = slice(N_STATE + c * SCAN_LANES, N_STATE + (c + 1) * SCAN_LANES)
        ar = are_ref[:, re]
        ai = aim_ref[:, re]
        x = [st_ref[0:half, re], st_ref[0:half, im], st_ref[half:bsz, re], st_ref[half:bsz, im]]
        for t in range(T_SCAN):
            r0 = t * bsz
            for s in range(2):
                rs = slice(r0 + s * half, r0 + (s + 1) * half)
                xr, xi = x[2 * s], x[2 * s + 1]
                x[2 * s] = ar * xr - ai * xi + bu_ref[rs, re]
                x[2 * s + 1] = ar * xi + ai * xr + bu_ref[rs, im]
            xs_ref[r0:r0 + bsz, re] = jnp.concatenate([x[0], x[2]], axis=0).astype(BF16)
            xs_ref[r0:r0 + bsz, im] = jnp.concatenate([x[1], x[3]], axis=0).astype(BF16)
        st_ref[0:half, re] = x[0]
        st_ref[0:half, im] = x[1]
        st_ref[half:bsz, re] = x[2]
        st_ref[half:bsz, im] = x[3]

    def c_proj(c):
        xm = jnp.concatenate(
            [xs_ref[:, 512 * c:512 * (c + 1)], xs_ref[:, N_STATE + 512 * c:N_STATE + 512 * (c + 1)]],
            axis=1)
        return jnp.dot(xm, wc_ref[c], preferred_element_type=F32)

    n_chunks = N_STATE // SCAN_LANES
    ys = []
    b_proj(0)
    for c in range(n_chunks):
        if c + 1 < n_chunks:
            b_proj(c + 1)
        recurrence(c)
        ys.append(c_proj(c))
    y = jnp.concatenate(ys, axis=1) + dskip_ref[...] * u_tb
    y = jax.nn.gelu(y)
    z = jnp.dot(y.astype(BF16), wglu_ref[...], preferred_element_type=F32) + bglu_ref[...]
    ytb_ref[...] = y * jax.nn.sigmoid(z)
    for t in range(T_SCAN):
        y_ref[:, t, :] = ytb_ref[t * bsz:(t + 1) * bsz, :]


def _s5_scan(u, a_re, a_im, wb, wc, d_skip, w_glu, b_glu):
    bsz, seq, _ = u.shape
    rows = T_SCAN * bsz
    blk = lambda t: (0, t, 0)
    return pl.pallas_call(
        _s5_scan_kernel,
        grid=(seq // T_SCAN,),
        in_specs=[
            pl.BlockSpec((bsz, T_SCAN, D_SSM), blk),
            _const_spec((8, N_STATE)),
            _const_spec((8, N_STATE)),
            _const_spec((2 * N_STATE // 256, LANES, 256)),
            _const_spec((D_SSM // LANES, 1024, LANES)),
            _const_spec((1, D_SSM)),
            _const_spec((D_SSM, D_SSM)),
            _const_spec((1, D_SSM)),
        ],
        out_specs=pl.BlockSpec((bsz, T_SCAN, D_SSM), blk),
        out_shape=jax.ShapeDtypeStruct((bsz, seq, D_SSM), F32),
        scratch_shapes=[
            pltpu.VMEM((rows, D_SSM), F32),
            pltpu.VMEM((rows, 2 * N_STATE), F32),
            pltpu.VMEM((rows, 2 * N_STATE), BF16),
            pltpu.VMEM((bsz, 2 * N_STATE), F32),
            pltpu.VMEM((rows, D_SSM), F32),
        ],
        compiler_params=pltpu.CompilerParams(
            dimension_semantics=("arbitrary",), vmem_limit_bytes=VMEM_LIMIT),
        name="s5_scan",
    )(u, a_re, a_im, wb, wc, d_skip, w_glu, b_glu)


NEG_BIG = -1e30


def _lane_tile_reduce(x, op):
    acc = x[:, :LANES]
    for i in range(1, x.shape[1] // LANES):
        acc = op(acc, x[:, i * LANES:(i + 1) * LANES])
    return acc


def _fox_kernel(q_ref, kt_ref, v_ref, crow_ref, mask_ref, o_ref, s_ref, p_ref, va_ref):
    seq = q_ref.shape[0]
    n_pairs = q_ref.shape[1] // LANES
    lane = lax.broadcasted_iota(jnp.int32, (1, LANES), 1)
    head_lanes = (lane < HEAD_DIM, lane >= HEAD_DIM)
    for hp in range(n_pairs):
        v2 = v_ref[:, hp * LANES:(hp + 1) * LANES]
        for head in range(2):
            va_ref[2 * hp + head] = jnp.where(head_lanes[head], v2, jnp.ones_like(v2))

    def scores(qi, hp, head, slot):
        q0 = qi * TQ
        klen = q0 + TQ
        q2 = q_ref[q0:q0 + TQ, hp * LANES:(hp + 1) * LANES]
        qh = jnp.where(head_lanes[head], q2, jnp.zeros_like(q2))
        crow = crow_ref[2 * hp + head:2 * hp + head + 1, 0:klen]
        rk = crow[:, klen - 1:klen] - crow
        m_run = None
        for c0 in range(0, klen, TQ):
            s = jnp.dot(qh, kt_ref[hp * LANES:(hp + 1) * LANES, c0:c0 + TQ],
                        preferred_element_type=F32) + rk[:, c0:c0 + TQ]
            if c0 == q0:
                s = s + mask_ref[...]
            s_ref[slot, :, c0:c0 + TQ] = s
            cm = _lane_tile_reduce(s, jnp.maximum)
            m_run = cm if m_run is None else jnp.maximum(m_run, cm)
        return jnp.max(m_run, axis=1, keepdims=True)

    def weighted_values(qi, hp, head, slot, m):
        klen = (qi + 1) * TQ
        for c0 in range(0, klen, TQ):
            p_ref[slot, :, c0:c0 + TQ] = jnp.exp2(s_ref[slot, :, c0:c0 + TQ] - m).astype(BF16)
        o = jnp.dot(p_ref[slot, :, 0:klen], va_ref[2 * hp + head, 0:klen, :], preferred_element_type=F32)
        l_col = HEAD_DIM * (1 - head)
        return o / o[:, l_col:l_col + 1]

    work = [(qi, hp, head) for qi in reversed(range(seq // TQ)) for hp in range(n_pairs)
            for head in range(2)]
    slots = s_ref.shape[0]
    m_next = scores(*work[0], 0)
    outs = []
    for i, (qi, hp, head) in enumerate(work):
        m_cur = m_next
        if i + 1 < len(work):
            m_next = scores(*work[i + 1], (i + 1) % slots)
        outs.append(weighted_values(qi, hp, head, i % slots, m_cur))
        if head == 1:
            o_ref[qi * TQ:(qi + 1) * TQ, hp * LANES:(hp + 1) * LANES] = jnp.where(
                head_lanes[0], outs[0], outs[1]).astype(o_ref.dtype)
            outs = []


def _fox_attention(q, kt, v, crow, mask):
    bsz, seq, _ = q.shape
    return pl.pallas_call(
        _fox_kernel,
        grid=(bsz,),
        in_specs=[
            pl.BlockSpec((None, seq, D_ATTN), lambda b: (b, 0, 0)),
            pl.BlockSpec((None, D_ATTN, seq), lambda b: (b, 0, 0)),
            pl.BlockSpec((None, seq, D_ATTN), lambda b: (b, 0, 0)),
            pl.BlockSpec((None, N_HEADS, seq), lambda b: (b, 0, 0)),
            _const_spec((TQ, TQ)),
        ],
        out_specs=pl.BlockSpec((None, seq, D_ATTN), lambda b: (b, 0, 0)),
        out_shape=jax.ShapeDtypeStruct((bsz, seq, D_ATTN), BF16),
        scratch_shapes=[
            pltpu.VMEM((2, TQ, seq), F32),
            pltpu.VMEM((2, TQ, seq), BF16),
            pltpu.VMEM((N_HEADS, seq, LANES), BF16),
        ],
        compiler_params=pltpu.CompilerParams(
            dimension_semantics=("arbitrary",), vmem_limit_bytes=VMEM_LIMIT),
        name="fox_attention",
    )(q, kt, v, crow, mask)


def _out_ffn_kernel(x_ref, ys_ref, ya_ref, gs_ref, ga_ref, wout_ref, gf_ref, wg_ref, wv_ref,
                    cw_ref, cb_ref, wd_ref, o_ref, act_ref, prev_ref, ext_ref):
    @pl.when(pl.program_id(1) == 0)
    def _():
        prev_ref[...] = jnp.zeros_like(prev_ref)

    ms = _rms(ys_ref[...], gs_ref[...])
    ma = _rms(ya_ref[...].astype(F32), ga_ref[...])
    mixed = jnp.concatenate([ms, ma], axis=1).astype(BF16)
    h = x_ref[...] + jnp.dot(mixed, wout_ref[...], preferred_element_type=F32)
    hn = _rms(h, gf_ref[...]).astype(BF16)

    rows = ROWS_FFN

    def conv(u, off, slot):
        outs = []
        for j in range(FF_CHUNK // LANES):
            cols = slice(off + j * LANES, off + (j + 1) * LANES)
            uj = u[:, j * LANES:(j + 1) * LANES]
            ext_ref[slot + j, 0:8, :] = prev_ref[:, cols]
            ext_ref[slot + j, 8:8 + rows, :] = uj
            prev_ref[:, cols] = uj[rows - 8:, :]
            w = cw_ref[:, cols]
            outs.append(ext_ref[slot + j, 6:6 + rows, :] * w[0:1] + ext_ref[slot + j, 7:7 + rows, :] * w[1:2]
                        + uj * w[2:3] + cb_ref[:, cols])
        return jnp.concatenate(outs, axis=1)

    tiles = FF_CHUNK // LANES
    for c in range(D_FF_PAD // FF_CHUNK):
        lo = c * FF_CHUNK
        slot = (c % 2) * 2 * tiles
        ug = jnp.dot(hn, wg_ref[:, lo:lo + FF_CHUNK], preferred_element_type=F32)
        uv = jnp.dot(hn, wv_ref[:, lo:lo + FF_CHUNK], preferred_element_type=F32)
        cg = conv(ug, lo, slot)
        cv = conv(uv, D_FF_PAD + lo, slot + tiles)
        act_ref[:, lo:lo + FF_CHUNK] = (cg * jax.nn.sigmoid(cg) * cv).astype(BF16)
    o_ref[...] = h + jnp.dot(act_ref[...], wd_ref[...], preferred_element_type=F32)


def _out_ffn(x, y_ssm, y_attn, gs, ga, w_out, gf, wg, wv, cw, cb, wd):
    bsz, seq, _ = x.shape
    row = lambda b, t: (b, t, 0)
    return pl.pallas_call(
        _out_ffn_kernel,
        grid=(bsz, seq // ROWS_FFN),
        in_specs=[
            pl.BlockSpec((None, ROWS_FFN, D_MODEL), row),
            pl.BlockSpec((None, ROWS_FFN, D_SSM), row),
            pl.BlockSpec((None, ROWS_FFN, D_ATTN), row),
            _const_spec((1, D_SSM)),
            _const_spec((1, D_ATTN)),
            _const_spec((D_MODEL, D_MODEL)),
            _const_spec((1, D_MODEL)),
            _const_spec((D_MODEL, D_FF_PAD)),
            _const_spec((D_MODEL, D_FF_PAD)),
            _const_spec((3, 2 * D_FF_PAD)),
            _const_spec((1, 2 * D_FF_PAD)),
            _const_spec((D_FF_PAD, D_MODEL)),
        ],
        out_specs=pl.BlockSpec((None, ROWS_FFN, D_MODEL), row),
        out_shape=jax.ShapeDtypeStruct((bsz, seq, D_MODEL), F32),
        scratch_shapes=[
            pltpu.VMEM((ROWS_FFN, D_FF_PAD), BF16),
            pltpu.VMEM((8, 2 * D_FF_PAD), F32),
            pltpu.VMEM((4 * FF_CHUNK // LANES, ROWS_FFN + 8, LANES), F32),
        ],
        compiler_params=pltpu.CompilerParams(
            dimension_semantics=("arbitrary", "arbitrary"), vmem_limit_bytes=VMEM_LIMIT),
        name="out_ffn",
    )(x, y_ssm, y_attn, gs, ga, w_out, gf, wg, wv, cw, cb, wd)


def _pad_cols(a, n):
    return jnp.pad(a, ((0, 0), (0, n - a.shape[1])))


def _layer(h, norm_mix, w_in, b_forget, lam_re, lam_im, b_re, b_im, c_re, c_im, d_skip, log_dt,
           w_glu, b_glu, q_norm, k_norm, norm_out_ssm, norm_out_attn, w_out, norm_ffn, w_up,
           conv_w, conv_b, w_down):
    a_re, a_im, wb, wc = _s5_discretize(lam_re, lam_im, log_dt, b_re, b_im, c_re, c_im)

    w_cat = _pad_cols(w_in, W_IN_COLS).astype(BF16)
    bf_pad = _pad_cols(b_forget.reshape(1, N_HEADS), LANES)
    qg = jnp.tile(q_norm * (HEAD_DIM ** -0.5 * LOG2E), N_HEADS).reshape(1, D_ATTN)
    kg = jnp.tile(k_norm, N_HEADS).reshape(1, D_ATTN)
    head_id = jnp.arange(D_ATTN) // HEAD_DIM
    ones_bd = (head_id[:, None] == head_id[None, :]).astype(BF16)
    r = jnp.arange(ROWS_IN)
    tri = (r[:, None] >= r[None, :]).astype(BF16)

    u, q, kt, v, crow = _in_proj(h, norm_mix.reshape(1, D_MODEL), w_cat, bf_pad, qg, kg, ones_bd, tri)
    y_ssm = _s5_scan(u, a_re, a_im, wb, wc, d_skip.reshape(1, D_SSM), w_glu.astype(BF16),
                     b_glu.reshape(1, D_SSM))
    rq = jnp.arange(TQ)
    causal_bias = jnp.where(rq[:, None] >= rq[None, :], 0.0, NEG_BIG).astype(F32)
    y_attn = _fox_attention(q, kt, v, crow, causal_bias)

    wg = _pad_cols(w_up[:, :D_FF], D_FF_PAD).astype(BF16)
    wv = _pad_cols(w_up[:, D_FF:], D_FF_PAD).astype(BF16)
    cw = jnp.concatenate([_pad_cols(conv_w[:, :D_FF], D_FF_PAD), _pad_cols(conv_w[:, D_FF:], D_FF_PAD)], axis=1)
    cbp = jnp.concatenate([_pad_cols(conv_b[None, :D_FF], D_FF_PAD), _pad_cols(conv_b[None, D_FF:], D_FF_PAD)], axis=1)
    wd = jnp.pad(w_down, ((0, D_FF_PAD - D_FF), (0, 0))).astype(BF16)
    return _out_ffn(h, y_ssm, y_attn, norm_out_ssm.reshape(1, D_SSM), norm_out_attn.reshape(1, D_ATTN),
                    w_out.astype(BF16), norm_ffn.reshape(1, D_MODEL), wg, wv, cw, cbp, wd)


def kernel(x, norm_mix, w_in, b_forget, lam_re, lam_im, b_re, b_im, c_re, c_im, d_skip, log_dt,
           w_glu, b_glu, q_norm, k_norm, norm_out_ssm, norm_out_attn, w_out, norm_ffn, w_up, conv_w,
           conv_b, w_down):
    h = x
    for l in range(norm_mix.shape[0]):
        h = _layer(h, norm_mix[l], w_in[l], b_forget[l], lam_re[l], lam_im[l], b_re[l], b_im[l],
                   c_re[l], c_im[l], d_skip[l], log_dt[l], w_glu[l], b_glu[l], q_norm[l], k_norm[l],
                   norm_out_ssm[l], norm_out_attn[l], w_out[l], norm_ffn[l], w_up[l], conv_w[l],
                   conv_b[l], w_down[l])
    return h
```

```python
import functools

import jax
import jax.numpy as jnp
from jax import lax
from jax.experimental import pallas as pl
from jax.experimental.pallas import tpu as pltpu

F32 = jnp.float32
BF16 = jnp.bfloat16

D_MODEL = 1024
D_SSM = 512
N_GROUPS = 32
GROUP = 16
STATE = 64
N_STATE = N_GROUPS * STATE
D_ATTN = 512
N_HEADS = 8
HEAD_DIM = 64
D_FF = 2752
EPS = 1e-6
LOG2E = 1.4426950408889634

LANES = 128
D_FF_PAD = 2816
W_IN_COLS = D_SSM + 3 * D_ATTN + LANES

ROWS_IN = 512
T_SCAN = 32
TQ = 256
ROWS_FFN = 512
FF_CHUNK = 256

VMEM_LIMIT = 56 * 1024 * 1024


def _const_spec(shape):
    nd = len(shape)
    return pl.BlockSpec(shape, lambda *_: (0,) * nd, pipeline_mode=pl.Buffered(1))


def _rms(x, g):
    return x * lax.rsqrt(jnp.mean(x * x, axis=-1, keepdims=True) + EPS) * g


def _s5_discretize_kernel(lr_ref, li_ref, ldt_ref, br_ref, bi_ref, cr_ref, ci_ref, bmask_ref, cmask_ref,
                          expand_ref, are_ref, aim_ref, wb_ref, wc_ref):
    lr = lr_ref[...]
    li = li_ref[...]
    dt = jnp.exp(ldt_ref[...])
    mag = jnp.exp(lr * dt)
    ab_re = mag * jnp.cos(li * dt)
    ab_im = mag * jnp.sin(li * dt)
    nr = ab_re - 1.0
    ni = ab_im
    den = lr * lr + li * li
    q_re = (nr * lr + ni * li) / den
    q_im = (ni * lr - nr * li) / den
    are_ref[...] = jnp.broadcast_to(ab_re, are_ref.shape)
    aim_ref[...] = jnp.broadcast_to(ab_im, aim_ref.shape)
    br = br_ref[...]
    bi = bi_ref[...]
    bb = (q_re * br - q_im * bi, q_re * bi + q_im * br)
    n_tiles = N_STATE // 256
    reps = LANES // GROUP
    for part in range(2):
        for j in range(n_tiles):
            blk = bb[part][:, 256 * j:256 * (j + 1)]
            wb_ref[part * n_tiles + j] = (jnp.concatenate([blk] * reps, axis=0) * bmask_ref[j % 2]).astype(BF16)
    for m in range(D_SSM // LANES):
        rows = slice(512 * m, 512 * (m + 1))
        cr = jnp.dot(cr_ref[rows, :].astype(BF16), expand_ref[...], preferred_element_type=F32)
        ci = jnp.dot(ci_ref[rows, :].astype(BF16), expand_ref[...], preferred_element_type=F32)
        wc_ref[m, 0:512, :] = (cr * cmask_ref[...]).astype(BF16)
        wc_ref[m, 512:1024, :] = (-ci * cmask_ref[...]).astype(BF16)


def _s5_discretize(lam_re, lam_im, log_dt, b_re, b_im, c_re, c_im):
    lr = lam_re.reshape(1, N_STATE)
    li = lam_im.reshape(1, N_STATE)
    ldt = jnp.repeat(log_dt, STATE).reshape(1, N_STATE)
    br = jnp.transpose(b_re, (2, 0, 1)).reshape(GROUP, N_STATE)
    bi = jnp.transpose(b_im, (2, 0, 1)).reshape(GROUP, N_STATE)
    cr = jnp.transpose(c_re, (0, 2, 1)).reshape(N_STATE, GROUP)
    ci = jnp.transpose(c_im, (0, 2, 1)).reshape(N_STATE, GROUP)
    n_tiles = N_STATE // 256
    r = jnp.arange(LANES)[:, None] // GROUP
    c = jnp.arange(256)[None, :] // STATE
    bmask = jnp.stack([(r == c), (r == c + 256 // STATE)]).astype(F32)
    cmask = ((jnp.arange(512)[:, None] // STATE) == (jnp.arange(LANES)[None, :] // GROUP)).astype(F32)
    expand = (jnp.arange(GROUP)[:, None] == (jnp.arange(LANES)[None, :] % GROUP)).astype(BF16)
    return pl.pallas_call(
        _s5_discretize_kernel,
        out_shape=(jax.ShapeDtypeStruct((8, N_STATE), F32),
                   jax.ShapeDtypeStruct((8, N_STATE), F32),
                   jax.ShapeDtypeStruct((2 * n_tiles, LANES, 256), BF16),
                   jax.ShapeDtypeStruct((D_SSM // LANES, 1024, LANES), BF16)),
        name="s5_discretize",
    )(lr, li, ldt, br, bi, cr, ci, bmask, cmask, expand)


def _in_proj_kernel(x_ref, g_ref, w_ref, bf_ref, qg_ref, kg_ref, ones_ref, tri_ref,
                    u_ref, q_ref, kt_ref, v_ref, crow_ref, carry_ref):
    @pl.when(pl.program_id(1) == 0)
    def _():
        carry_ref[...] = jnp.zeros_like(carry_ref)

    hn = _rms(x_ref[...], g_ref[...]).astype(BF16)
    proj = jnp.dot(hn, w_ref[...], preferred_element_type=F32)
    u_ref[...] = proj[:, :D_SSM]
    q = proj[:, D_SSM:D_SSM + D_ATTN]
    k = proj[:, D_SSM + D_ATTN:D_SSM + 2 * D_ATTN]
    v = proj[:, D_SSM + 2 * D_ATTN:D_SSM + 3 * D_ATTN]
    f = proj[:, D_SSM + 3 * D_ATTN:]

    def head_norm(z, g):
        ss = jnp.dot((z * z).astype(BF16), ones_ref[...], preferred_element_type=F32)
        return z * lax.rsqrt(ss * (1.0 / HEAD_DIM) + EPS) * g

    q_ref[...] = head_norm(q, qg_ref[...]).astype(BF16)
    kt_ref[...] = head_norm(k, kg_ref[...]).T.astype(BF16)
    v_ref[...] = v.astype(BF16)

    z = f + bf_ref[...]
    logf = (jnp.minimum(z, 0.0) - jnp.log1p(jnp.exp(-jnp.abs(z)))).T[:N_HEADS, :]
    hi = logf.astype(BF16)
    r1 = logf - hi.astype(F32)
    mid = r1.astype(BF16)
    lo = (r1 - mid.astype(F32)).astype(BF16)
    parts = jnp.dot(jnp.concatenate([hi, mid, lo], axis=0), tri_ref[...], preferred_element_type=F32)
    cum = (parts[0:N_HEADS] + parts[N_HEADS:2 * N_HEADS] + parts[2 * N_HEADS:]) + carry_ref[:, 0:1]
    carry_ref[...] = jnp.broadcast_to(cum[:, ROWS_IN - 1:ROWS_IN], carry_ref.shape)
    crow_ref[...] = cum * LOG2E


def _in_proj(x, norm_g, w_cat, bf_pad, qg, kg, ones_bd, tri):
    bsz, seq, _ = x.shape
    nt = seq // ROWS_IN
    row = lambda b, t: (b, t, 0)
    return pl.pallas_call(
        _in_proj_kernel,
        grid=(bsz, nt),
        in_specs=[
            pl.BlockSpec((None, ROWS_IN, D_MODEL), row),
            _const_spec((1, D_MODEL)),
            _const_spec((D_MODEL, W_IN_COLS)),
            _const_spec((1, LANES)),
            _const_spec((1, D_ATTN)),
            _const_spec((1, D_ATTN)),
            _const_spec((D_ATTN, D_ATTN)),
            _const_spec((ROWS_IN, ROWS_IN)),
        ],
        out_specs=[
            pl.BlockSpec((None, ROWS_IN, D_SSM), row),
            pl.BlockSpec((None, ROWS_IN, D_ATTN), row),
            pl.BlockSpec((None, D_ATTN, ROWS_IN), lambda b, t: (b, 0, t)),
            pl.BlockSpec((None, ROWS_IN, D_ATTN), row),
            pl.BlockSpec((None, N_HEADS, ROWS_IN), lambda b, t: (b, 0, t)),
        ],
        out_shape=[
            jax.ShapeDtypeStruct((bsz, seq, D_SSM), F32),
            jax.ShapeDtypeStruct((bsz, seq, D_ATTN), BF16),
            jax.ShapeDtypeStruct((bsz, D_ATTN, seq), BF16),
            jax.ShapeDtypeStruct((bsz, seq, D_ATTN), BF16),
            jax.ShapeDtypeStruct((bsz, N_HEADS, seq), F32),
        ],
        scratch_shapes=[pltpu.VMEM((8, LANES), F32)],
        compiler_params=pltpu.CompilerParams(
            dimension_semantics=("arbitrary", "arbitrary"), vmem_limit_bytes=VMEM_LIMIT),
        name="in_proj",
    )(x, norm_g, w_cat, bf_pad, qg, kg, ones_bd, tri)


SCAN_LANES = 512


def _s5_scan_kernel(u_ref, are_ref, aim_ref, wb_ref, wc_ref, dskip_ref, wglu_ref, bglu_ref,
                    y_ref, utb_ref, bu_ref, xs_ref, st_ref, ytb_ref):
    bsz = u_ref.shape[0]
    rows = T_SCAN * bsz

    @pl.when(pl.program_id(0) == 0)
    def _():
        st_ref[...] = jnp.zeros_like(st_ref)

    for t in range(T_SCAN):
        utb_ref[t * bsz:(t + 1) * bsz, :] = u_ref[:, t, :]
    u_tb = utb_ref[...]
    ub = u_tb.astype(BF16)

    n_tiles = N_STATE // 256
    half = bsz // 2

    def b_proj(c):
        for j in (2 * c, 2 * c + 1, n_tiles + 2 * c, n_tiles + 2 * c + 1):
            bu_ref[:, 256 * j:256 * (j + 1)] = jnp.dot(
                ub[:, LANES * c:LANES * (c + 1)], wb_ref[j], preferred_element_type=F32)

    def recurrence(c):
        re = slice(c * SCAN_LANES, (c + 1) * SCAN_LANES)
        im = slice(N_STATE + c * SCAN_LANES, N_STATE + (c + 1) * SCAN_LANES)
        ar = are_ref[:, re]
        ai = aim_ref[:, re]
        x = [st_ref[0:half, re], st_ref[0:half, im], st_ref[half:bsz, re], st_ref[half:bsz, im]]
        for t in range(T_SCAN):
            r0 = t * bsz
            for s in range(2):
                rs = slice(r0 + s * half, r0 + (s + 1) * half)
                xr, xi = x[2 * s], x[2 * s + 1]
                x[2 * s] = ar * xr - ai * xi + bu_ref[rs, re]
                x[2 * s + 1] = ar * xi + ai * xr + bu_ref[rs, im]
            xs_ref[r0:r0 + bsz, re] = jnp.concatenate([x[0], x[2]], axis=0).astype(BF16)
            xs_ref[r0:r0 + bsz, im] = jnp.concatenate([x[1], x[3]], axis=0).astype(BF16)
        st_ref[0:half, re] = x[0]
        st_ref[0:half, im] = x[1]
        st_ref[half:bsz, re] = x[2]
        st_ref[half:bsz, im] = x[3]

    def c_proj(c):
        xm = jnp.concatenate(
            [xs_ref[:, 512 * c:512 * (c + 1)], xs_ref[:, N_STATE + 512 * c:N_STATE + 512 * (c + 1)]],
            axis=1)
        return jnp.dot(xm, wc_ref[c], preferred_element_type=F32)

    n_chunks = N_STATE // SCAN_LANES
    ys = []
    b_proj(0)
    for c in range(n_chunks):
        if c + 1 < n_chunks:
            b_proj(c + 1)
        recurrence(c)
        ys.append(c_proj(c))
    y = jnp.concatenate(ys, axis=1) + dskip_ref[...] * u_tb
    y = jax.nn.gelu(y)
    z = jnp.dot(y.astype(BF16), wglu_ref[...], preferred_element_type=F32) + bglu_ref[...]
    ytb_ref[...] = y * jax.nn.sigmoid(z)
    for t in range(T_SCAN):
        y_ref[:, t, :] = ytb_ref[t * bsz:(t + 1) * bsz, :]


def _s5_scan(u, a_re, a_im, wb, wc, d_skip, w_glu, b_glu):
    bsz, seq, _ = u.shape
    rows = T_SCAN * bsz
    blk = lambda t: (0, t, 0)
    return pl.pallas_call(
        _s5_scan_kernel,
        grid=(seq // T_SCAN,),
        in_specs=[
            pl.BlockSpec((bsz, T_SCAN, D_SSM), blk),
            _const_spec((8, N_STATE)),
            _const_spec((8, N_STATE)),
            _const_spec((2 * N_STATE // 256, LANES, 256)),
            _const_spec((D_SSM // LANES, 1024, LANES)),
            _const_spec((1, D_SSM)),
            _const_spec((D_SSM, D_SSM)),
            _const_spec((1, D_SSM)),
        ],
        out_specs=pl.BlockSpec((bsz, T_SCAN, D_SSM), blk),
        out_shape=jax.ShapeDtypeStruct((bsz, seq, D_SSM), F32),
        scratch_shapes=[
            pltpu.VMEM((rows, D_SSM), F32),
            pltpu.VMEM((rows, 2 * N_STATE), F32),
            pltpu.VMEM((rows, 2 * N_STATE), BF16),
            pltpu.VMEM((bsz, 2 * N_STATE), F32),
            pltpu.VMEM((rows, D_SSM), F32),
        ],
        compiler_params=pltpu.CompilerParams(
            dimension_semantics=("arbitrary",), vmem_limit_bytes=VMEM_LIMIT),
        name="s5_scan",
    )(u, a_re, a_im, wb, wc, d_skip, w_glu, b_glu)


NEG_BIG = -1e30


def _lane_tile_reduce(x, op):
    acc = x[:, :LANES]
    for i in range(1, x.shape[1] // LANES):
        acc = op(acc, x[:, i * LANES:(i + 1) * LANES])
    return acc


def _fox_kernel(q_ref, kt_ref, v_ref, crow_ref, mask_ref, o_ref, s_ref, p_ref, va_ref):
    seq = q_ref.shape[0]
    n_pairs = q_ref.shape[1] // LANES
    lane = lax.broadcasted_iota(jnp.int32, (1, LANES), 1)
    head_lanes = (lane < HEAD_DIM, lane >= HEAD_DIM)
    for hp in range(n_pairs):
        v2 = v_ref[:, hp * LANES:(hp + 1) * LANES]
        for head in range(2):
            va_ref[2 * hp + head] = jnp.where(head_lanes[head], v2, jnp.ones_like(v2))

    def scores(qi, hp, head, slot):
        q0 = qi * TQ
        klen = q0 + TQ
        q2 = q_ref[q0:q0 + TQ, hp * LANES:(hp + 1) * LANES]
        qh = jnp.where(head_lanes[head], q2, jnp.zeros_like(q2))
        crow = crow_ref[2 * hp + head:2 * hp + head + 1, 0:klen]
        rk = crow[:, klen - 1:klen] - crow
        m_run = None
        for c0 in range(0, klen, TQ):
            s = jnp.dot(qh, kt_ref[hp * LANES:(hp + 1) * LANES, c0:c0 + TQ],
                        preferred_element_type=F32) + rk[:, c0:c0 + TQ]
            if c0 == q0:
                s = s + mask_ref[...]
            s_ref[slot, :, c0:c0 + TQ] = s
            cm = _lane_tile_reduce(s, jnp.maximum)
            m_run = cm if m_run is None else jnp.maximum(m_run, cm)
        return jnp.max(m_run, axis=1, keepdims=True)

    def weighted_values(qi, hp, head, slot, m):
        klen = (qi + 1) * TQ
        for c0 in range(0, klen, TQ):
            p_ref[slot, :, c0:c0 + TQ] = jnp.exp2(s_ref[slot, :, c0:c0 + TQ] - m).astype(BF16)
        o = jnp.dot(p_ref[slot, :, 0:klen], va_ref[2 * hp + head, 0:klen, :], preferred_element_type=F32)
        l_col = HEAD_DIM * (1 - head)
        return o / o[:, l_col:l_col + 1]

    work = [(qi, hp, head) for qi in reversed(range(seq // TQ)) for hp in range(n_pairs)
            for head in range(2)]
    slots = s_ref.shape[0]
    m_next = scores(*work[0], 0)
    outs = []
    for i, (qi, hp, head) in enumerate(work):
        m_cur = m_next
        if i + 1 < len(work):
            m_next = scores(*work[i + 1], (i + 1) % slots)
        outs.append(weighted_values(qi, hp, head, i % slots, m_cur))
        if head == 1:
            o_ref[qi * TQ:(qi + 1) * TQ, hp * LANES:(hp + 1) * LANES] = jnp.where(
                head_lanes[0], outs[0], outs[1]).astype(o_ref.dtype)
            outs = []


def _fox_attention(q, kt, v, crow, mask):
    bsz, seq, _ = q.shape
    return pl.pallas_call(
        _fox_kernel,
        grid=(bsz,),
        in_specs=[
            pl.BlockSpec((None, seq, D_ATTN), lambda b: (b, 0, 0)),
            pl.BlockSpec((None, D_ATTN, seq), lambda b: (b, 0, 0)),
            pl.BlockSpec((None, seq, D_ATTN), lambda b: (b, 0, 0)),
            pl.BlockSpec((None, N_HEADS, seq), lambda b: (b, 0, 0)),
            _const_spec((TQ, TQ)),
        ],
        out_specs=pl.BlockSpec((None, seq, D_ATTN), lambda b: (b, 0, 0)),
        out_shape=jax.ShapeDtypeStruct((bsz, seq, D_ATTN), BF16),
        scratch_shapes=[
            pltpu.VMEM((2, TQ, seq), F32),
            pltpu.VMEM((2, TQ, seq), BF16),
            pltpu.VMEM((N_HEADS, seq, LANES), BF16),
        ],
        compiler_params=pltpu.CompilerParams(
            dimension_semantics=("arbitrary",), vmem_limit_bytes=VMEM_LIMIT),
        name="fox_attention",
    )(q, kt, v, crow, mask)


def _out_ffn_kernel(x_ref, ys_ref, ya_ref, gs_ref, ga_ref, wout_ref, gf_ref, wg_ref, wv_ref,
                    cw_ref, cb_ref, wd_ref, o_ref, act_ref, prev_ref, ext_ref):
    @pl.when(pl.program_id(1) == 0)
    def _():
        prev_ref[...] = jnp.zeros_like(prev_ref)

    ms = _rms(ys_ref[...], gs_ref[...])
    ma = _rms(ya_ref[...].astype(F32), ga_ref[...])
    mixed = jnp.concatenate([ms, ma], axis=1).astype(BF16)
    h = x_ref[...] + jnp.dot(mixed, wout_ref[...], preferred_element_type=F32)
    hn = _rms(h, gf_ref[...]).astype(BF16)

    rows = ROWS_FFN

    def conv(u, off, slot):
        outs = []
        for j in range(FF_CHUNK // LANES):
            cols = slice(off + j * LANES, off + (j + 1) * LANES)
            uj = u[:, j * LANES:(j + 1) * LANES]
            ext_ref[slot + j, 0:8, :] = prev_ref[:, cols]
            ext_ref[slot + j, 8:8 + rows, :] = uj
            prev_ref[:, cols] = uj[rows - 8:, :]
            w = cw_ref[:, cols]
            outs.append(ext_ref[slot + j, 6:6 + rows, :] * w[0:1] + ext_ref[slot + j, 7:7 + rows, :] * w[1:2]
                        + uj * w[2:3] + cb_ref[:, cols])
        return jnp.concatenate(outs, axis=1)

    tiles = FF_CHUNK // LANES
    for c in range(D_FF_PAD // FF_CHUNK):
        lo = c * FF_CHUNK
        slot = (c % 2) * 2 * tiles
        ug = jnp.dot(hn, wg_ref[:, lo:lo + FF_CHUNK], preferred_element_type=F32)
        uv = jnp.dot(hn, wv_ref[:, lo:lo + FF_CHUNK], preferred_element_type=F32)
        cg = conv(ug, lo, slot)
        cv = conv(uv, D_FF_PAD + lo, slot + tiles)
        act_ref[:, lo:lo + FF_CHUNK] = (cg * jax.nn.sigmoid(cg) * cv).astype(BF16)
    o_ref[...] = h + jnp.dot(act_ref[...], wd_ref[...], preferred_element_type=F32)


def _out_ffn(x, y_ssm, y_attn, gs, ga, w_out, gf, wg, wv, cw, cb, wd):
    bsz, seq, _ = x.shape
    row = lambda b, t: (b, t, 0)
    return pl.pallas_call(
        _out_ffn_kernel,
        grid=(bsz, seq // ROWS_FFN),
        in_specs=[
            pl.BlockSpec((None, ROWS_FFN, D_MODEL), row),
            pl.BlockSpec((None, ROWS_FFN, D_SSM), row),
            pl.BlockSpec((None, ROWS_FFN, D_ATTN), row),
            _const_spec((1, D_SSM)),
            _const_spec((1, D_ATTN)),
            _const_spec((D_MODEL, D_MODEL)),
            _const_spec((1, D_MODEL)),
            _const_spec((D_MODEL, D_FF_PAD)),
            _const_spec((D_MODEL, D_FF_PAD)),
            _const_spec((3, 2 * D_FF_PAD)),
            _const_spec((1, 2 * D_FF_PAD)),
            _const_spec((D_FF_PAD, D_MODEL)),
        ],
        out_specs=pl.BlockSpec((None, ROWS_FFN, D_MODEL), row),
        out_shape=jax.ShapeDtypeStruct((bsz, seq, D_MODEL), F32),
        scratch_shapes=[
            pltpu.VMEM((ROWS_FFN, D_FF_PAD), BF16),
            pltpu.VMEM((8, 2 * D_FF_PAD), F32),
            pltpu.VMEM((4 * FF_CHUNK // LANES, ROWS_FFN + 8, LANES), F32),
        ],
        compiler_params=pltpu.CompilerParams(
            dimension_semantics=("arbitrary", "arbitrary"), vmem_limit_bytes=VMEM_LIMIT),
        name="out_ffn",
    )(x, y_ssm, y_attn, gs, ga, w_out, gf, wg, wv, cw, cb, wd)


def _pad_cols(a, n):
    return jnp.pad(a, ((0, 0), (0, n - a.shape[1])))


def _layer(h, norm_mix, w_in, b_forget, lam_re, lam_im, b_re, b_im, c_re, c_im, d_skip, log_dt,
           w_glu, b_glu, q_norm, k_norm, norm_out_ssm, norm_out_attn, w_out, norm_ffn, w_up,
           conv_w, conv_b, w_down):
    a_re, a_im, wb, wc = _s5_discretize(lam_re, lam_im, log_dt, b_re, b_im, c_re, c_im)

    w_cat = _pad_cols(w_in.astype(BF16), W_IN_COLS)
    bf_pad = _pad_cols(b_forget.reshape(1, N_HEADS), LANES)
    qg = jnp.tile(q_norm * (HEAD_DIM ** -0.5 * LOG2E), N_HEADS).reshape(1, D_ATTN)
    kg = jnp.tile(k_norm, N_HEADS).reshape(1, D_ATTN)
    head_id = jnp.arange(D_ATTN) // HEAD_DIM
    ones_bd = (head_id[:, None] == head_id[None, :]).astype(BF16)
    r = jnp.arange(ROWS_IN)
    tri = (r[:, None] <= r[None, :]).astype(BF16)

    u, q, kt, v, crow = _in_proj(h, norm_mix.reshape(1, D_MODEL), w_cat, bf_pad, qg, kg, ones_bd, tri)
    y_ssm = _s5_scan(u, a_re, a_im, wb, wc, d_skip.reshape(1, D_SSM), w_glu.astype(BF16),
                     b_glu.reshape(1, D_SSM))
    rq = jnp.arange(TQ)
    causal_bias = jnp.where(rq[:, None] >= rq[None, :], 0.0, NEG_BIG).astype(F32)
    y_attn = _fox_attention(q, kt, v, crow, causal_bias)

    wg = _pad_cols(w_up[:, :D_FF].astype(BF16), D_FF_PAD)
    wv = _pad_cols(w_up[:, D_FF:].astype(BF16), D_FF_PAD)
    cw = jnp.concatenate([_pad_cols(conv_w[:, :D_FF], D_FF_PAD), _pad_cols(conv_w[:, D_FF:], D_FF_PAD)], axis=1)
    cbp = jnp.concatenate([_pad_cols(conv_b[None, :D_FF], D_FF_PAD), _pad_cols(conv_b[None, D_FF:], D_FF_PAD)], axis=1)
    wd = jnp.pad(w_down.astype(BF16), ((0, D_FF_PAD - D_FF), (0, 0)))
    return _out_ffn(h, y_ssm, y_attn, norm_out_ssm.reshape(1, D_SSM), norm_out_attn.reshape(1, D_ATTN),
                    w_out.astype(BF16), norm_ffn.reshape(1, D_MODEL), wg, wv, cw, cbp, wd)


def kernel(x, norm_mix, w_in, b_forget, lam_re, lam_im, b_re, b_im, c_re, c_im, d_skip, log_dt,
           w_glu, b_glu, q_norm, k_norm, norm_out_ssm, norm_out_attn, w_out, norm_ffn, w_up, conv_w,
           conv_b, w_down):
    h = x
    for l in range(norm_mix.shape[0]):
        h = _layer(h, norm_mix[l], w_in[l], b_forget[l], lam_re[l], lam_im[l], b_re[l], b_im[l],
                   c_re[l], c_im[l], d_skip[l], log_dt[l], w_glu[l], b_glu[l], q_norm[l], k_norm[l],
                   norm_out_ssm[l], norm_out_attn[l], w_out[l], norm_ffn[l], w_up[l], conv_w[l],
                   conv_b[l], w_down[l])
    return h
```
